```python
import math
import jax, jax.numpy as jnp
from jax import lax
import numpy as np

D_MODEL = 1024
BATCH = 16
SEQ = 2048
DEPTH = 1
DEC_BATCH = 128
DEC_SEQ = 8
PAST_LEN = 8192
PAGE_SIZE = 128

SSM_CH = 16
SSM_WIDTH = D_MODEL
SSM_GROUPS = SSM_WIDTH // SSM_CH
SSM_STATE = 64
HEAD_DIM = 64
HEADS_PER_GROUP = D_MODEL // 128
WINDOWS = (128, 512, 2048)
DILATIONS = (1, 4, 16)
N_ATTN_GROUPS = len(WINDOWS)
ATTN_WIDTH = N_ATTN_GROUPS * HEADS_PER_GROUP * HEAD_DIM
ATTN_OUT = HEADS_PER_GROUP * HEAD_DIM
PROJ_WIDTH = SSM_WIDTH + 3 * ATTN_WIDTH + 2 * D_MODEL
PEER_HEADS = 8
PEER_TOPK = 16
N_KEYS = 128
N_EXPERTS = N_KEYS * N_KEYS
PEER_HALF = 128
TOKEN_BLOCK = 128
EPS = 1e-6

kernel_name = "hybrid_s5_dilated_peer_decode_step"


def rmsnorm(x, g):
    xf = x.astype(jnp.float32)
    xf = xf * lax.rsqrt(jnp.mean(xf * xf, axis=-1, keepdims=True) + EPS)
    return (xf * g.astype(jnp.float32)).astype(x.dtype)


def split_projection(xn, w_in):
    B, S = xn.shape[:2]
    p = jnp.einsum('bsd,df->bsf', xn, w_in)
    o1 = SSM_WIDTH
    o2 = o1 + ATTN_WIDTH
    o3 = o2 + ATTN_WIDTH
    o4 = o3 + ATTN_WIDTH
    o5 = o4 + D_MODEL
    u, q, k, v, ga, gb = jnp.split(p, [o1, o2, o3, o4, o5], axis=-1)
    heads = lambda t: t.reshape(B, S, N_ATTN_GROUPS, HEADS_PER_GROUP, HEAD_DIM)
    return u.reshape(B, S, SSM_GROUPS, SSM_CH), heads(q), heads(k), heads(v), ga, gb


def _scan_combine(e1, e2):
    a1r, a1i, b1r, b1i = e1
    a2r, a2i, b2r, b2i = e2
    return (a1r * a2r - a1i * a2i,
            a1r * a2i + a1i * a2r,
            a2r * b1r - a2i * b1i + b2r,
            a2r * b1i + a2i * b1r + b2i)


def ssm_branch(u, h0_re, h0_im, lam_re, lam_im, log_dt, b_re, b_im, c_re, c_im, d_skip):
    f32 = jnp.float32
    B, S = u.shape[:2]
    uf = u.astype(f32)
    lr, li = lam_re.astype(f32), lam_im.astype(f32)
    dt = jnp.exp(log_dt.astype(f32))[:, None]
    mag = jnp.exp(lr * dt)
    ang = li * dt
    ab_re, ab_im = mag * jnp.cos(ang), mag * jnp.sin(ang)
    den = lr * lr + li * li
    f_re = ((ab_re - 1.0) * lr + ab_im * li) / den
    f_im = (ab_im * lr - (ab_re - 1.0) * li) / den
    br, bi = b_re.astype(f32), b_im.astype(f32)
    bb_re = f_re[..., None] * br - f_im[..., None] * bi
    bb_im = f_re[..., None] * bi + f_im[..., None] * br
    x_re = jnp.einsum('bsgc,gnc->bsgn', uf, bb_re)
    x_im = jnp.einsum('bsgc,gnc->bsgn', uf, bb_im)
    if h0_re is not None:
        h0r, h0i = h0_re.astype(f32), h0_im.astype(f32)
        x_re = x_re.at[:, 0].add(ab_re * h0r - ab_im * h0i)
        x_im = x_im.at[:, 0].add(ab_re * h0i + ab_im * h0r)
    a_re = jnp.broadcast_to(ab_re, (1, S) + ab_re.shape)
    a_im = jnp.broadcast_to(ab_im, (1, S) + ab_im.shape)
    _, _, h_re, h_im = lax.associative_scan(_scan_combine, (a_re, a_im, x_re, x_im), axis=1)
    y = (jnp.einsum('bsgn,gcn->bsgc', h_re, c_re.astype(f32))
         - jnp.einsum('bsgn,gcn->bsgc', h_im, c_im.astype(f32))
         + d_skip.astype(f32) * uf)
    return y.astype(u.dtype), h_re[:, -1].astype(u.dtype), h_im[:, -1].astype(u.dtype)


def dilated_attn_prompt(q, k, v, dil, n_steps):
    B, S, H, E = q.shape
    L = S // dil
    blk = n_steps
    nb = -(-L // blk)
    Lp = nb * blk

    def sub(t):
        t = t.reshape(B, L, dil, H, E).transpose(0, 2, 1, 3, 4)
        t = jnp.pad(t, ((0, 0), (0, 0), (0, Lp - L), (0, 0), (0, 0)))
        return t.reshape(B, dil, nb, blk, H, E)

    qb, kb, vb = sub(q), sub(k), sub(v)

    def with_prev(t):
        prev = jnp.pad(t[:, :, :-1], ((0, 0), (0, 0), (1, 0), (0, 0), (0, 0), (0, 0)))
        return jnp.concatenate([prev, t], axis=3)

    kk, vv = with_prev(kb), with_prev(vb)
    s = jnp.einsum('brnqhe,brnkhe->brnhqk', qb, kk,
                   preferred_element_type=jnp.float32) * (E ** -0.5)
    qi = jnp.arange(blk)[:, None]
    ki = jnp.arange(2 * blk)[None, :]
    dist = qi + blk - ki
    rel = (dist >= 0) & (dist <= n_steps)
    exists = (jnp.arange(nb)[:, None, None] > 0) | (ki[None] >= blk)
    mask = rel[None] & exists
    s = jnp.where(mask[None, None, :, None], s, -jnp.inf)
    m = jnp.max(s, axis=-1, keepdims=True)
    p = jnp.exp(s - m)
    den = jnp.sum(p, axis=-1, keepdims=True)
    o = jnp.einsum('brnhqk,brnkhe->brnqhe', (p / den).astype(vv.dtype), vv)
    lse = (m + jnp.log(den))[..., 0]
    o = o.reshape(B, dil, Lp, H, E)[:, :, :L].transpose(0, 2, 1, 3, 4).reshape(B, S, H, E)
    lse = lse.transpose(0, 1, 2, 4, 3).reshape(B, dil, Lp, H)[:, :, :L]
    lse = lse.transpose(0, 2, 1, 3).reshape(B, S, H)
    return o, lse


def dilated_attn_sample(q, kv_all, dil, n_steps):
    B, T, H, E = q.shape
    WB = kv_all.shape[1] - T
    idx = WB + jnp.arange(T)[:, None] - dil * jnp.arange(n_steps + 1)[None, :]
    valid = idx >= 0
    g = jnp.take(kv_all, jnp.maximum(idx, 0), axis=1)
    s = jnp.einsum('bqhe,bqkhe->bhqk', q, g[:, :, :, 0],
                   preferred_element_type=jnp.float32) * (E ** -0.5)
    s = jnp.where(valid[None, None], s, -jnp.inf)
    m = jnp.max(s, axis=-1, keepdims=True)
    p = jnp.exp(s - m)
    den = jnp.sum(p, axis=-1, keepdims=True)
    o = jnp.einsum('bhqk,bqkhe->bqhe', (p / den).astype(kv_all.dtype), g[:, :, :, 1])
    lse = (m + jnp.log(den))[..., 0].transpose(0, 2, 1)
    return o, lse


def peer_ffn(x, w_qp, sub_keys, u_tab, v_tab):
    lead = x.shape[:-1]
    xt = x.reshape(-1, D_MODEL)
    n = xt.shape[0]
    n_blk = -(-n // TOKEN_BLOCK)
    xt = jnp.pad(xt, ((0, n_blk * TOKEN_BLOCK - n), (0, 0)))

    def block(xb):
        q = jnp.einsum('td,dq->tq', xb, w_qp).reshape(TOKEN_BLOCK, PEER_HEADS, 2, PEER_HALF)
        s = jnp.einsum('thpe,hpke->thpk', q, sub_keys, preferred_element_type=jnp.float32)
        sv, si = lax.top_k(s, PEER_TOPK)
        cand = (sv[:, :, 0, :, None] + sv[:, :, 1, None, :]).reshape(TOKEN_BLOCK, PEER_HEADS, PEER_TOPK * PEER_TOPK)
        cidx = (si[:, :, 0, :, None] * N_KEYS + si[:, :, 1, None, :]).reshape(TOKEN_BLOCK, PEER_HEADS, PEER_TOPK * PEER_TOPK)
        best, pos = lax.top_k(cand, PEER_TOPK)
        eidx = jnp.take_along_axis(cidx, pos, axis=-1).reshape(TOKEN_BLOCK, PEER_HEADS * PEER_TOPK)
        gate = jax.nn.softmax(best, axis=-1).reshape(TOKEN_BLOCK, PEER_HEADS * PEER_TOPK)
        ue = jnp.take(u_tab, eidx, axis=0)
        act = jax.nn.gelu(jnp.einsum('td,tkd->tk', xb, ue, preferred_element_type=jnp.float32))
        ve = jnp.take(v_tab, eidx, axis=0)
        return jnp.einsum('tk,tkd->td', (gate * act).astype(xb.dtype), ve)

    out = lax.map(block, xt.reshape(n_blk, TOKEN_BLOCK, D_MODEL))
    return out.reshape(-1, D_MODEL)[:n].reshape(*lead, D_MODEL)


def decoder_layer(x, h0_re, h0_im, kv_caches, g_mix, w_in, lam_re, lam_im, log_dt, b_re, b_im,
                  c_re, c_im, d_skip, w_glu_a, w_glu_b, w_attn_proj, w_out, g_ffn, w_qp,
                  sub_keys, u_tab, v_tab):
    B, S, _ = x.shape
    xn = rmsnorm(x, g_mix)
    u, q, k, v, ga, gb = split_projection(xn, w_in)
    y_ssm, hT_re, hT_im = ssm_branch(u, h0_re, h0_im, lam_re, lam_im, log_dt, b_re, b_im, c_re, c_im, d_skip)
    hs = jax.nn.gelu(y_ssm.reshape(B, S, SSM_WIDTH))
    branch_a = jnp.einsum('bsc,cd->bsd', hs, w_glu_a) * jax.nn.sigmoid(jnp.einsum('bsc,cd->bsd', hs, w_glu_b))
    outs, lses, kv_new = [], [], []
    for gi in range(N_ATTN_GROUPS):
        win, dil = WINDOWS[gi], DILATIONS[gi]
        qg, kg, vg = q[:, :, gi], k[:, :, gi], v[:, :, gi]
        kv = jnp.stack([kg, vg], axis=2)
        if kv_caches is None:
            o, l = dilated_attn_prompt(qg, kg, vg, dil, win // dil)
            kv_new.append(kv[:, S - min(win, S):])
        else:
            o, l = dilated_attn_sample(qg, jnp.concatenate([kv_caches[gi], kv], axis=1), dil, win // dil)
            kv_new.append(kv)
        outs.append(o)
        lses.append(l)
    wts = jax.nn.softmax(jnp.stack(lses, axis=0), axis=0)[..., None]
    attn = jnp.sum(wts * jnp.stack(outs, axis=0).astype(jnp.float32), axis=0)
    attn = attn.astype(x.dtype).reshape(B, S, ATTN_OUT)
    branch_b = jnp.einsum('bsc,cd->bsd', attn, w_attn_proj)
    mix = jax.nn.sigmoid(ga) * branch_a + jax.nn.sigmoid(gb) * branch_b
    x = x + jnp.einsum('bsd,de->bse', mix, w_out)
    x = x + peer_ffn(rmsnorm(x, g_ffn), w_qp, sub_keys, u_tab, v_tab)
    return x, hT_re, hT_im, kv_new


def setup_inputs(seed: int = 0) -> dict:
    key = jax.random.key(seed)
    ks = jax.random.split(key, 32)
    f32 = jnp.float32
    nrm = lambda kk, shape, scale: scale * jax.random.normal(kk, shape, f32)
    L = DEPTH
    wb = [min(w, PAST_LEN) for w in WINDOWS]
    lam_im = jnp.pi * jnp.arange(SSM_STATE, dtype=f32)[None, None, :] + nrm(ks[10], (L, SSM_GROUPS, SSM_STATE), 0.01)
    return {
        "x_prompt": nrm(ks[0], (BATCH, SEQ, D_MODEL), 1.0),
        "x_sample": nrm(ks[1], (DEC_BATCH, DEC_SEQ, D_MODEL), 1.0),
        "state_ssm_re": nrm(ks[2], (L, DEC_BATCH, SSM_GROUPS, SSM_STATE), 0.5),
        "state_ssm_im": nrm(ks[3], (L, DEC_BATCH, SSM_GROUPS, SSM_STATE), 0.5),
        "cache_kv_w128": nrm(ks[4], (L, DEC_BATCH, wb[0], 2, HEADS_PER_GROUP, HEAD_DIM), 1.0),
        "cache_kv_w512": nrm(ks[5], (L, DEC_BATCH, wb[1], 2, HEADS_PER_GROUP, HEAD_DIM), 1.0),
        "cache_kv_w2048": nrm(ks[6], (L, DEC_BATCH, wb[2], 2, HEADS_PER_GROUP, HEAD_DIM), 1.0),
        "g_mix": 1.0 + nrm(ks[7], (L, D_MODEL), 0.01),
        "w_in": nrm(ks[8], (L, D_MODEL, PROJ_WIDTH), D_MODEL ** -0.5),
        "lam_re": -0.5 + nrm(ks[9], (L, SSM_GROUPS, SSM_STATE), 0.01),
        "lam_im": lam_im,
        "log_dt": jax.random.uniform(ks[11], (L, SSM_GROUPS), f32, math.log(1e-3), math.log(1e-1)),
        "b_re": nrm(ks[12], (L, SSM_GROUPS, SSM_STATE, SSM_CH), (2 * SSM_CH) ** -0.5),
        "b_im": nrm(ks[13], (L, SSM_GROUPS, SSM_STATE, SSM_CH), (2 * SSM_CH) ** -0.5),
        "c_re": nrm(ks[14], (L, SSM_GROUPS, SSM_CH, SSM_STATE), (2 * SSM_STATE) ** -0.5),
        "c_im": nrm(ks[15], (L, SSM_GROUPS, SSM_CH, SSM_STATE), (2 * SSM_STATE) ** -0.5),
        "d_skip": nrm(ks[16], (L, SSM_GROUPS, SSM_CH), 1.0),
        "w_glu_a": nrm(ks[17], (L, SSM_WIDTH, D_MODEL), SSM_WIDTH ** -0.5),
        "w_glu_b": nrm(ks[18], (L, SSM_WIDTH, D_MODEL), SSM_WIDTH ** -0.5),
        "w_attn_proj": nrm(ks[19], (L, ATTN_OUT, D_MODEL), ATTN_OUT ** -0.5),
        "w_out": nrm(ks[20], (L, D_MODEL, D_MODEL), D_MODEL ** -0.5),
        "g_ffn": 1.0 + nrm(ks[21], (L, D_MODEL), 0.01),
        "w_qp": nrm(ks[22], (L, D_MODEL, PEER_HEADS * 2 * PEER_HALF), D_MODEL ** -0.5),
        "sub_keys": nrm(ks[23], (L, PEER_HEADS, 2, N_KEYS, PEER_HALF), PEER_HALF ** -0.5),
        "u_tab": nrm(ks[24], (L, N_EXPERTS, D_MODEL), D_MODEL ** -0.5),
        "v_tab": nrm(ks[25], (L, N_EXPERTS, D_MODEL), 0.3),
        "g_final": 1.0 + nrm(ks[26], (D_MODEL,), 0.01),
    }


def reference(x_prompt, x_sample, state_ssm_re, state_ssm_im, cache_kv_w128, cache_kv_w512,
              cache_kv_w2048, g_mix, w_in, lam_re, lam_im, log_dt, b_re, b_im, c_re, c_im, d_skip,
              w_glu_a, w_glu_b, w_attn_proj, w_out, g_ffn, w_qp, sub_keys, u_tab, v_tab, g_final):
    xp, xs = x_prompt, x_sample
    sre_p, sim_p, sre_s, sim_s = [], [], [], []
    kvp = [[] for _ in range(N_ATTN_GROUPS)]
    kvs = [[] for _ in range(N_ATTN_GROUPS)]
    for li in range(DEPTH):
        lp = (g_mix[li], w_in[li], lam_re[li], lam_im[li], log_dt[li], b_re[li], b_im[li],
              c_re[li], c_im[li], d_skip[li], w_glu_a[li], w_glu_b[li], w_attn_proj[li], w_out[li],
              g_ffn[li], w_qp[li], sub_keys[li], u_tab[li], v_tab[li])
        xp, hr, hi, kv_new_p = decoder_layer(xp, None, None, None, *lp)
        sre_p.append(hr)
        sim_p.append(hi)
        caches = (cache_kv_w128[li], cache_kv_w512[li], cache_kv_w2048[li])
        xs, hr, hi, kv_new_s = decoder_layer(xs, state_ssm_re[li], state_ssm_im[li], caches, *lp)
        sre_s.append(hr)
        sim_s.append(hi)
        for gi in range(N_ATTN_GROUPS):
            kvp[gi].append(kv_new_p[gi])
            kvs[gi].append(kv_new_s[gi])
    y_prompt = rmsnorm(xp, g_final)
    y_sample = rmsnorm(xs, g_final)
    return (y_prompt, y_sample,
            jnp.stack(sre_p), jnp.stack(sim_p),
            jnp.stack(kvp[0]), jnp.stack(kvp[1]), jnp.stack(kvp[2]),
            jnp.stack(sre_s), jnp.stack(sim_s),
            jnp.stack(kvs[0]), jnp.stack(kvs[1]), jnp.stack(kvs[2]))
```

```python
import functools
import math

import jax
import jax.numpy as jnp
from jax import lax
from jax.experimental import pallas as pl
from jax.experimental.pallas import tpu as pltpu

F32 = jnp.float32
BF16 = jnp.bfloat16

D_MODEL = 1024
SSM_CH = 16
SSM_GROUPS = D_MODEL // SSM_CH
SSM_STATE = 64
SSM_LANES = SSM_GROUPS * SSM_STATE
HEAD_DIM = 64
HEADS = 8
HEAD_LANES = HEADS * HEAD_DIM
N_GROUPS = 3
WINDOWS = (128, 512, 2048)
DILATIONS = (1, 4, 16)
N_STEPS = 128
Q_WIDTH = N_GROUPS * HEAD_LANES
KV_WIDTH = 2 * HEAD_LANES
PROJ_WIDTH = D_MODEL + 3 * Q_WIDTH + 2 * D_MODEL
PEER_HEADS = 8
PEER_TOPK = 16
N_KEYS = 128
N_EXPERTS = N_KEYS * N_KEYS
PEER_HALF = 128
EPS = 1e-6
NEG_INF = float("-inf")

SSM_BATCH_TILE = 16
SSM_BLOCK_GROUPS = 8
SSM_N_BLOCKS = SSM_GROUPS // SSM_BLOCK_GROUPS
SSM_BLOCK_STATE = SSM_BLOCK_GROUPS * SSM_STATE
SCAN_LANES = 512

VMEM_LIMIT_BYTES = 56 * 1024 * 1024


def _params(semantics):
    return pltpu.CompilerParams(dimension_semantics=semantics, vmem_limit_bytes=VMEM_LIMIT_BYTES)


def _const_spec(shape):
    zeros = (0,) * len(shape)
    return pl.BlockSpec(shape, lambda *_: zeros)


def _gelu(x):
    return jax.nn.gelu(x, approximate=True)


_SEG_U = (0, D_MODEL)
_SEG_Q = (D_MODEL, D_MODEL + Q_WIDTH)
_SEG_KV0 = D_MODEL + Q_WIDTH
_SEG_GA = _SEG_KV0 + N_GROUPS * KV_WIDTH
_SEG_GB = _SEG_GA + D_MODEL


def _in_proj_body(x_ref, g_ref, w_ref, u_ref, q_ref, kv0_ref, kv1_ref, kv2_ref, ga_ref, gb_ref, *, nb, ts):
    x = x_ref[...].reshape(nb * ts, D_MODEL)
    ms = jnp.mean(x * x, axis=-1, keepdims=True)
    xn = ((x * lax.rsqrt(ms + EPS)) * g_ref[...]).astype(BF16)

    def seg(a, b):
        return jnp.dot(xn, w_ref[:, a:b], preferred_element_type=F32)

    u = seg(*_SEG_U)
    for k in range(nb):
        u_ref[:, k * D_MODEL:(k + 1) * D_MODEL] = u[k * ts:(k + 1) * ts]
    q_ref[...] = (seg(*_SEG_Q) * (HEAD_DIM ** -0.5)).reshape(nb, ts, Q_WIDTH).astype(q_ref.dtype)
    for gi, ref in enumerate((kv0_ref, kv1_ref, kv2_ref)):
        a = _SEG_KV0 + gi * KV_WIDTH
        ref[...] = seg(a, a + KV_WIDTH).reshape(nb, ts, KV_WIDTH)
    ga_ref[...] = seg(_SEG_GA, _SEG_GB).reshape(nb, ts, D_MODEL)
    gb_ref[...] = seg(_SEG_GB, PROJ_WIDTH).reshape(nb, ts, D_MODEL)


def _in_proj(x, g_mix, w_perm, *, nb, ts, q_dtype):
    B, S, _ = x.shape
    row = lambda w: pl.BlockSpec((nb, ts, w), lambda b, s: (b, s, 0))
    f32_rows = lambda w: jax.ShapeDtypeStruct((B, S, w), F32)
    return pl.pallas_call(
        functools.partial(_in_proj_body, nb=nb, ts=ts),
        grid=(B // nb, S // ts),
        in_specs=[row(D_MODEL), _const_spec((1, D_MODEL)),
                  pl.BlockSpec((D_MODEL, PROJ_WIDTH), lambda b, s: (0, 0), pipeline_mode=pl.Buffered(1))],
        out_specs=(pl.BlockSpec((ts, nb * D_MODEL), lambda b, s: (s, b)), row(Q_WIDTH),
                   row(KV_WIDTH), row(KV_WIDTH), row(KV_WIDTH), row(D_MODEL), row(D_MODEL)),
        out_shape=(jax.ShapeDtypeStruct((S, B * D_MODEL), F32),
                   jax.ShapeDtypeStruct((B, S, Q_WIDTH), q_dtype),
                   f32_rows(KV_WIDTH), f32_rows(KV_WIDTH), f32_rows(KV_WIDTH),
                   f32_rows(D_MODEL), f32_rows(D_MODEL)),
        compiler_params=_params(("parallel", "parallel")),
        name="in_proj",
    )(x, g_mix.reshape(1, D_MODEL), w_perm)


def _ssm_body(u_ref, h0re_ref, h0im_ref, wx_ref, cy_ref, are_ref, aim_ref, dsk_ref,
              hs_ref, hre_ref, him_ref, xre, xim, st_re, st_im, *, ts):
    si = pl.program_id(1)
    rows = ts * SSM_BATCH_TILE

    @pl.when(si == 0)
    def _():
        st_re[...] = h0re_ref[...]
        st_im[...] = h0im_ref[...]

    u = u_ref[...].reshape(rows, D_MODEL)
    ub = u.astype(BF16)
    for j in range(SSM_N_BLOCKS):
        r = jnp.dot(ub[:, j * 128:(j + 1) * 128], wx_ref[j], preferred_element_type=F32)
        cols = slice(j * SSM_BLOCK_STATE, (j + 1) * SSM_BLOCK_STATE)
        xre[:, cols] = r[:, :SSM_BLOCK_STATE]
        xim[:, cols] = r[:, SSM_BLOCK_STATE:]

    for c in range(SSM_LANES // SCAN_LANES):
        cols = slice(c * SCAN_LANES, (c + 1) * SCAN_LANES)
        a_re = jnp.broadcast_to(are_ref[:, cols], (SSM_BATCH_TILE, SCAN_LANES))
        a_im = jnp.broadcast_to(aim_ref[:, cols], (SSM_BATCH_TILE, SCAN_LANES))

        def step(t, carry, cols=cols, a_re=a_re, a_im=a_im):
            h_re, h_im = carry
            r0 = pl.multiple_of(t * SSM_BATCH_TILE, SSM_BATCH_TILE)
            n_re = a_re * h_re - a_im * h_im + xre[pl.ds(r0, SSM_BATCH_TILE), cols]
            n_im = a_re * h_im + a_im * h_re + xim[pl.ds(r0, SSM_BATCH_TILE), cols]
            xre[pl.ds(r0, SSM_BATCH_TILE), cols] = n_re
            xim[pl.ds(r0, SSM_BATCH_TILE), cols] = n_im
            return n_re, n_im

        h_re, h_im = lax.fori_loop(0, ts, step, (st_re[:, cols], st_im[:, cols]), unroll=min(ts, 8))
        st_re[:, cols] = h_re
        st_im[:, cols] = h_im

    for j in range(SSM_N_BLOCKS):
        cols = slice(j * SSM_BLOCK_STATE, (j + 1) * SSM_BLOCK_STATE)
        h = jnp.concatenate([xre[:, cols], xim[:, cols]], axis=1).astype(BF16)
        y = jnp.dot(h, cy_ref[j], preferred_element_type=F32)
        ch = slice(j * 128, (j + 1) * 128)
        y = y + dsk_ref[:, ch] * u[:, ch]
        hs_ref[:, :, ch] = _gelu(y).reshape(ts, SSM_BATCH_TILE, 128)

    @pl.when(si == pl.num_programs(1) - 1)
    def _():
        hre_ref[...] = st_re[...]
        him_ref[...] = st_im[...]


def _ssm(u_t, h0_re, h0_im, wx, cy, a_re, a_im, dsk, *, ts):
    S = u_t.shape[0]
    B = u_t.shape[1] // D_MODEL
    nbt = B // SSM_BATCH_TILE
    u4 = u_t.reshape(S, nbt, SSM_BATCH_TILE, D_MODEL)
    tile = pl.BlockSpec((ts, None, SSM_BATCH_TILE, D_MODEL), lambda b, s: (s, b, 0, 0))
    state = pl.BlockSpec((SSM_BATCH_TILE, SSM_LANES), lambda b, s: (b, 0))
    rows = ts * SSM_BATCH_TILE
    hs, hre, him = pl.pallas_call(
        functools.partial(_ssm_body, ts=ts),
        grid=(nbt, S // ts),
        in_specs=[tile, state, state,
                  _const_spec((SSM_N_BLOCKS, 128, 2 * SSM_BLOCK_STATE)),
                  _const_spec((SSM_N_BLOCKS, 2 * SSM_BLOCK_STATE, 128)),
                  _const_spec((1, SSM_LANES)), _const_spec((1, SSM_LANES)), _const_spec((1, D_MODEL))],
        out_specs=(tile, state, state),
        out_shape=(jax.ShapeDtypeStruct(u4.shape, F32),
                   jax.ShapeDtypeStruct((B, SSM_LANES), F32), jax.ShapeDtypeStruct((B, SSM_LANES), F32)),
        scratch_shapes=[pltpu.VMEM((rows, SSM_LANES), F32), pltpu.VMEM((rows, SSM_LANES), F32),
                        pltpu.VMEM((SSM_BATCH_TILE, SSM_LANES), F32), pltpu.VMEM((SSM_BATCH_TILE, SSM_LANES), F32)],
        compiler_params=_params(("parallel", "arbitrary")),
        name="ssm",
    )(u4, h0_re, h0_im, wx, cy, a_re, a_im, dsk)
    return hs.reshape(S, B * D_MODEL), hre, him


def _ssm_weights(lam_re, lam_im, log_dt, b_re, b_im, c_re, c_im, d_skip):
    dt = jnp.exp(log_dt)[:, None]
    mag = jnp.exp(lam_re * dt)
    ang = lam_im * dt
    ab_re, ab_im = mag * jnp.cos(ang), mag * jnp.sin(ang)
    den = lam_re * lam_re + lam_im * lam_im
    f_re = ((ab_re - 1.0) * lam_re + ab_im * lam_im) / den
    f_im = (ab_im * lam_re - (ab_re - 1.0) * lam_im) / den
    bb_re = f_re[..., None] * b_re - f_im[..., None] * b_im
    bb_im = f_re[..., None] * b_im + f_im[..., None] * b_re
    eye = jnp.eye(SSM_BLOCK_GROUPS, dtype=F32)

    def in_block(bb):
        bb = bb.reshape(SSM_N_BLOCKS, SSM_BLOCK_GROUPS, SSM_STATE, SSM_CH)
        return jnp.einsum('jgnc,gh->jgchn', bb, eye).reshape(SSM_N_BLOCKS, 128, SSM_BLOCK_STATE)

    def out_block(cc):
        cc = cc.reshape(SSM_N_BLOCKS, SSM_BLOCK_GROUPS, SSM_CH, SSM_STATE)
        return jnp.einsum('jgcn,gh->jgnhc', cc, eye).reshape(SSM_N_BLOCKS, SSM_BLOCK_STATE, 128)

    wx = jnp.concatenate([in_block(bb_re), in_block(bb_im)], axis=2).astype(BF16)
    cy = jnp.concatenate([out_block(c_re), out_block(-c_im)], axis=1).astype(BF16)
    return (wx, cy, ab_re.reshape(1, SSM_LANES), ab_im.reshape(1, SSM_LANES), d_skip.reshape(1, D_MODEL))


def _head_masks():
    lane = lax.broadcasted_iota(jnp.int32, (1, 128), 1)
    return [(lane // HEAD_DIM) == hh for hh in range(2)]


def _attn_body(q_ref, kc_ref, kp_ref, o_ref, lse_ref):
    n = pl.program_id(2)
    q = q_ref[0]
    kvc = kc_ref[0]
    kvp = kp_ref[0]
    kcat = jnp.concatenate([kvp[:, :HEAD_LANES], kvc[:, :HEAD_LANES]], axis=0).astype(BF16)
    vcat = jnp.concatenate([kvp[:, HEAD_LANES:], kvc[:, HEAD_LANES:]], axis=0).astype(BF16)
    qi = lax.broadcasted_iota(jnp.int32, (N_STEPS, 2 * N_STEPS), 0)
    ki = lax.broadcasted_iota(jnp.int32, (N_STEPS, 2 * N_STEPS), 1)
    first_key = jnp.where(n > 0, 0, N_STEPS)
    mask = (ki >= jnp.maximum(qi, first_key)) & (ki <= qi + N_STEPS)
    hm = _head_masks()
    lane = lax.broadcasted_iota(jnp.int32, (1, 128), 1)
    lse_tile = jnp.zeros((N_STEPS, 128), F32)
    for hp in range(HEADS // 2):
        ch = slice(hp * 128, (hp + 1) * 128)
        qp, kp, vp = q[:, ch], kcat[:, ch], vcat[:, ch]
        outs = []
        for hh in range(2):
            qm = jnp.where(hm[hh], qp, jnp.zeros_like(qp))
            s = lax.dot_general(qm, kp, (((1,), (1,)), ((), ())), preferred_element_type=F32)
            s = jnp.where(mask, s, NEG_INF)
            m = jnp.max(s, axis=-1, keepdims=True)
            p = jnp.exp(s - m)
            den = jnp.sum(p, axis=-1, keepdims=True)
            outs.append(jnp.dot((p / den).astype(BF16), vp, preferred_element_type=F32))
            lse = m + jnp.log(den)
            lse_tile = jnp.where((lane // 16) == (2 * hp + hh), lse, lse_tile)
        o_ref[0, :, ch] = jnp.where(hm[0], outs[0], outs[1])
    lse_ref[0] = lse_tile


def _attn_prompt(q, kv, gi):
    B, S, _ = q.shape
    d = DILATIONS[gi]
    L = S // d
    nblk = L // N_STEPS
    qv = q.reshape(B, L, d * Q_WIDTH)
    kvv = kv.reshape(B, L, d * KV_WIDTH)
    o, lse = pl.pallas_call(
        _attn_body,
        grid=(B, d, nblk),
        in_specs=[pl.BlockSpec((1, N_STEPS, HEAD_LANES), lambda b, r, n: (b, n, r * N_GROUPS + gi)),
                  pl.BlockSpec((1, N_STEPS, KV_WIDTH), lambda b, r, n: (b, n, r)),
                  pl.BlockSpec((1, N_STEPS, KV_WIDTH), lambda b, r, n: (b, jnp.maximum(n - 1, 0), r))],
        out_specs=(pl.BlockSpec((1, N_STEPS, HEAD_LANES), lambda b, r, n: (b, n, r)),
                   pl.BlockSpec((1, N_STEPS, 128), lambda b, r, n: (b, n, r))),
        out_shape=(jax.ShapeDtypeStruct((B, L, d * HEAD_LANES), F32),
                   jax.ShapeDtypeStruct((B, L, d * 128), F32)),
        compiler_params=_params(("parallel", "parallel", "arbitrary")),
        name=f"attn_prompt_g{gi}",
    )(qv, kvv, kvv)
    return o.reshape(B * S, HEAD_LANES), lse.reshape(B * S, 128)


def _combine_body(o0_ref, o1_ref, o2_ref, l0_ref, l1_ref, l2_ref, ex_ref, attn_ref):
    ls = [l0_ref[...], l1_ref[...], l2_ref[...]]
    m = jnp.maximum(jnp.maximum(ls[0], ls[1]), ls[2])
    es = [jnp.exp(l - m) for l in ls]
    den = es[0] + es[1] + es[2]
    acc = None
    for e, o_ref in zip(es, (o0_ref, o1_ref, o2_ref)):
        w = e / den
        w_hi = w.astype(BF16)
        w_lo = (w - w_hi.astype(F32)).astype(BF16)
        wx = (jnp.dot(w_hi, ex_ref[...], preferred_element_type=F32)
              + jnp.dot(w_lo, ex_ref[...], preferred_element_type=F32))
        term = wx * o_ref[...]
        acc = term if acc is None else acc + term
    attn_ref[...] = acc.astype(BF16)


def _head_expand_matrix():
    src = jnp.arange(128)[:, None]
    dst = jnp.arange(HEAD_LANES)[None, :]
    return (src == (dst // HEAD_DIM) * 16).astype(BF16)


def _combine(os_, lses, *, tm):
    T = os_[0].shape[0]
    orow = pl.BlockSpec((tm, HEAD_LANES), lambda i: (i, 0))
    lrow = pl.BlockSpec((tm, 128), lambda i: (i, 0))
    return pl.pallas_call(
        _combine_body,
        grid=(T // tm,),
        in_specs=[orow, orow, orow, lrow, lrow, lrow, _const_spec((128, HEAD_LANES))],
        out_specs=orow,
        out_shape=jax.ShapeDtypeStruct((T, HEAD_LANES), BF16),
        compiler_params=_params(("parallel",)),
        name="attn_combine",
    )(*os_, *lses, _head_expand_matrix())


SAMPLE_T = 8
SAMPLE_ROWS = N_STEPS + SAMPLE_T
SAMPLE_SLOTS = N_GROUPS * SAMPLE_T


def _attn_sample_body(q_ref, n0_ref, n1_ref, n2_ref, c0_ref, c1_ref, c2_ref, seg_ref, segt_ref,
                      attn_ref, prod, sbuf):
    q = q_ref[0].astype(F32)
    new_refs = (n0_ref, n1_ref, n2_ref)
    cache_refs = (c0_ref, c1_ref, c2_ref)

    def keys_values(gi, t):
        r = t % DILATIONS[gi]
        base = r * KV_WIDTH
        kc = cache_refs[gi][0, :, base:base + HEAD_LANES]
        vc = cache_refs[gi][0, :, base + HEAD_LANES:base + KV_WIDTH]
        return kc, vc

    for gi in range(N_GROUPS):
        kn = new_refs[gi][0, :, :HEAD_LANES]
        for t in range(SAMPLE_T):
            qt = q[t:t + 1, gi * HEAD_LANES:(gi + 1) * HEAD_LANES]
            kc, _ = keys_values(gi, t)
            r0 = (gi * SAMPLE_T + t) * SAMPLE_ROWS
            prod[r0:r0 + N_STEPS, :] = kc * qt
            prod[r0 + N_STEPS:r0 + SAMPLE_ROWS, :] = kn * qt
    sbuf[...] = jnp.dot(prod[...].astype(BF16), seg_ref[...], preferred_element_type=F32)

    row = lax.broadcasted_iota(jnp.int32, (SAMPLE_ROWS, 1), 0)
    for t in range(SAMPLE_T):
        ss = []
        for gi in range(N_GROUPS):
            d = DILATIONS[gi]
            valid = (row >= t // d) & (row < N_STEPS)
            for tp in range(t % d, t + 1, d):
                valid = valid | (row == N_STEPS + tp)
            r0 = (gi * SAMPLE_T + t) * SAMPLE_ROWS
            ss.append(jnp.where(valid, sbuf[r0:r0 + SAMPLE_ROWS, :], NEG_INF))
        m = jnp.max(jnp.maximum(jnp.maximum(ss[0], ss[1]), ss[2]), axis=0, keepdims=True)
        ps = [jnp.exp(s - m) for s in ss]
        den = jnp.sum(ps[0] + ps[1] + ps[2], axis=0, keepdims=True)
        for gi in range(N_GROUPS):
            r0 = (gi * SAMPLE_T + t) * SAMPLE_ROWS
            sbuf[r0:r0 + SAMPLE_ROWS, :] = ps[gi] / den
    prod[...] = jnp.dot(sbuf[...].astype(BF16), segt_ref[...], preferred_element_type=F32)

    for t in range(SAMPLE_T):
        acc = jnp.zeros((1, HEAD_LANES), F32)
        for gi in range(N_GROUPS):
            _, vc = keys_values(gi, t)
            vn = new_refs[gi][0, :, HEAD_LANES:]
            r0 = (gi * SAMPLE_T + t) * SAMPLE_ROWS
            acc = acc + jnp.sum(prod[r0:r0 + N_STEPS, :] * vc, axis=0, keepdims=True)
            acc = acc + jnp.sum(prod[r0 + N_STEPS:r0 + SAMPLE_ROWS, :] * vn, axis=0, keepdims=True)
        attn_ref[0, t:t + 1, :] = acc


def _attn_sample(q, kv_new, caches):
    B = q.shape[0]
    views = []
    specs = []
    for gi, c in enumerate(caches):
        d = DILATIONS[gi]
        assert c.shape[1] == WINDOWS[gi], "decode caches must hold a full window"
        views.append(c.reshape(B, N_STEPS, d * KV_WIDTH))
        width = min(d, SAMPLE_T) * KV_WIDTH
        specs.append(pl.BlockSpec((1, N_STEPS, width), lambda b: (b, 0, 0)))
    head = jnp.arange(HEAD_LANES)[:, None] // HEAD_DIM
    seg = (head == jnp.arange(128)[None, :]).astype(BF16)
    new_spec = pl.BlockSpec((1, SAMPLE_T, KV_WIDTH), lambda b: (b, 0, 0))
    return pl.pallas_call(
        _attn_sample_body,
        grid=(B,),
        in_specs=[pl.BlockSpec((1, SAMPLE_T, Q_WIDTH), lambda b: (b, 0, 0)), new_spec, new_spec, new_spec,
                  *specs, _const_spec((HEAD_LANES, 128)), _const_spec((128, HEAD_LANES))],
        out_specs=pl.BlockSpec((1, SAMPLE_T, HEAD_LANES), lambda b: (b, 0, 0)),
        out_shape=jax.ShapeDtypeStruct((B, SAMPLE_T, HEAD_LANES), F32),
        scratch_shapes=[pltpu.VMEM((SAMPLE_SLOTS * SAMPLE_ROWS, HEAD_LANES), F32),
                        pltpu.VMEM((SAMPLE_SLOTS * SAMPLE_ROWS, 128), F32)],
        compiler_params=_params(("parallel",)),
        name="attn_sample",
    )(q, *kv_new, *views, seg, seg.T)


def _merge_body(hs_ref, attn_ref, ga_ref, gb_ref, x_ref, wa_ref, wb_ref, wp_ref, wo_ref, gf_ref, wq_ref, sk_ref,
                x1_ref, xn_ref, st_ref, *, nb, ts):
    rows = nb * ts
    hs = jnp.concatenate([hs_ref[:, k * D_MODEL:(k + 1) * D_MODEL] for k in range(nb)], axis=0).astype(BF16)
    dot = lambda a, w_ref: jnp.dot(a, w_ref[...], preferred_element_type=F32)
    branch_a = dot(hs, wa_ref) * jax.nn.sigmoid(dot(hs, wb_ref))
    branch_b = dot(attn_ref[...].reshape(rows, HEAD_LANES).astype(BF16), wp_ref)
    ga = ga_ref[...].reshape(rows, D_MODEL)
    gb = gb_ref[...].reshape(rows, D_MODEL)
    mix = jax.nn.sigmoid(ga) * branch_a + jax.nn.sigmoid(gb) * branch_b
    x1 = x_ref[...].reshape(rows, D_MODEL) + dot(mix.astype(BF16), wo_ref)
    x1_ref[...] = x1
    ms = jnp.mean(x1 * x1, axis=-1, keepdims=True)
    xn = ((x1 * lax.rsqrt(ms + EPS)) * gf_ref[...]).astype(BF16)
    xn_ref[...] = xn
    qp = dot(xn, wq_ref).astype(BF16)
    for hp in range(2 * PEER_HEADS):
        st_ref[hp] = lax.dot_general(sk_ref[hp], qp[:, hp * PEER_HALF:(hp + 1) * PEER_HALF],
                                     (((1,), (1,)), ((), ())), preferred_element_type=F32)


def _merge(hs_t, attn, ga, gb, x, wa, wb, wp, wo, g_ffn, wq, sk, *, nb, ts):
    B, S, _ = x.shape
    rows = nb * ts
    T = B * S
    nsb = S // ts
    row = lambda w: pl.BlockSpec((nb, ts, w), lambda b, s: (b, s, 0))
    flat = lambda w: pl.BlockSpec((rows, w), lambda b, s: (b * nsb + s, 0))
    return pl.pallas_call(
        functools.partial(_merge_body, nb=nb, ts=ts),
        grid=(B // nb, nsb),
        in_specs=[pl.BlockSpec((ts, nb * D_MODEL), lambda b, s: (s, b)), row(HEAD_LANES), row(D_MODEL),
                  row(D_MODEL), row(D_MODEL),
                  _const_spec((D_MODEL, D_MODEL)), _const_spec((D_MODEL, D_MODEL)),
                  _const_spec((HEAD_LANES, D_MODEL)), _const_spec((D_MODEL, D_MODEL)),
                  _const_spec((1, D_MODEL)), _const_spec((D_MODEL, 2 * PEER_HEADS * PEER_HALF)),
                  _const_spec((2 * PEER_HEADS, N_KEYS, PEER_HALF))],
        out_specs=(flat(D_MODEL), flat(D_MODEL),
                   pl.BlockSpec((2 * PEER_HEADS, N_KEYS, rows), lambda b, s: (0, 0, b * nsb + s))),
        out_shape=(jax.ShapeDtypeStruct((T, D_MODEL), F32), jax.ShapeDtypeStruct((T, D_MODEL), BF16),
                   jax.ShapeDtypeStruct((2 * PEER_HEADS, N_KEYS, T), F32)),
        compiler_params=_params(("parallel", "parallel")),
        name="merge",
    )(hs_t, attn, ga, gb, x, wa, wb, wp, wo, g_ffn.reshape(1, D_MODEL), wq, sk)


ROUTE_LANES = 128
_PAIR_SLABS = ((0, 0, 0, 1, 8), (0, 0, 8, 1, 8), (1, 0, 0, 1, 8), (2, 0, 0, 1, 5), (3, 0, 0, 1, 4),
               (4, 0, 0, 1, 3), (5, 0, 0, 1, 2), (6, 0, 0, 1, 2), (7, 0, 0, 1, 2), (8, 1, 0, 0, 8))


def _top16(s):
    idx = lax.broadcasted_iota(jnp.int32, s.shape, 0).astype(F32)
    cur = s
    rank = jnp.full(s.shape, float(PEER_TOPK), F32)
    vals = []
    for r in range(PEER_TOPK):
        m = jnp.max(cur, axis=0, keepdims=True)
        first = jnp.min(jnp.where(cur == m, idx, float(N_KEYS)), axis=0, keepdims=True)
        sel = idx == first
        rank = jnp.where(sel, float(r), rank)
        cur = jnp.where(sel, NEG_INF, cur)
        vals.append(m)
    return rank, vals


def _route_body(st_ref, c0_ref, l0_ref, b1_ref, r1_ref):
    sub = lax.broadcasted_iota(jnp.int32, (8, ROUTE_LANES), 0)

    def head(h, carry):
        s0 = st_ref[2 * h]
        s1 = st_ref[2 * h + 1]
        rank0, v0 = _top16(s0)
        rank1, v1 = _top16(s1)
        v0a, v0b = jnp.concatenate(v0[:8], axis=0), jnp.concatenate(v0[8:], axis=0)
        v1a, v1b = jnp.concatenate(v1[:8], axis=0), jnp.concatenate(v1[8:], axis=0)

        cands, cidx = [], []
        for (r0, r0s, r1, r1s, nrow) in _PAIR_SLABS:
            a = (v0b if r0 == 8 else v0[r0])
            b = (v1[0] if r1s == 0 else (v1a if r1 == 0 else v1b))
            c = a + b
            if nrow < 8:
                c = jnp.where(sub < nrow, c, NEG_INF)
            cands.append(c)
            cidx.append(((r0 + r0s * sub) * PEER_TOPK + (r1 + r1s * sub)).astype(F32))

        cur = list(cands)
        picked = [jnp.zeros((8, ROUTE_LANES), F32) for _ in cands]
        big = float(PEER_TOPK * PEER_TOPK)
        for _ in range(PEER_TOPK):
            m = functools.reduce(jnp.maximum, cur)
            m = jnp.max(m, axis=0, keepdims=True)
            first = functools.reduce(jnp.minimum, [jnp.where(c == m, ci, big) for c, ci in zip(cur, cidx)])
            first = jnp.min(first, axis=0, keepdims=True)
            for k in range(len(cur)):
                sel = cidx[k] == first
                picked[k] = jnp.where(sel, 1.0, picked[k])
                cur[k] = jnp.where(sel, NEG_INF, cur[k])

        top = v0[0] + v1[0]
        z = functools.reduce(lambda x, y: x + y,
                             [jnp.sum(pk * jnp.exp(c - top), axis=0, keepdims=True) for pk, c in zip(picked, cands)])
        counts = [jnp.sum(picked[0] + picked[1], axis=0, keepdims=True)]
        counts += [jnp.sum(picked[k], axis=0, keepdims=True) for k in range(2, 9)]
        counts += [picked[9][i:i + 1, :] for i in range(8)]
        l0 = jnp.zeros((N_KEYS, ROUTE_LANES), F32)
        for r in range(PEER_TOPK):
            l0 = jnp.where(rank0 == float(r), counts[r], l0)

        c0_ref[h] = jnp.exp(s0 - v0[0]) / z
        l0_ref[h] = l0
        b1_ref[h] = jnp.exp(s1 - v1[0]).astype(BF16)
        r1_ref[h] = rank1.astype(BF16)
        return carry

    lax.fori_loop(0, PEER_HEADS, head, 0)


def _route(st):
    T = st.shape[2]
    out = lambda dt: jax.ShapeDtypeStruct((PEER_HEADS, N_KEYS, T), dt)
    spec = pl.BlockSpec((PEER_HEADS, N_KEYS, ROUTE_LANES), lambda i: (0, 0, i))
    return pl.pallas_call(
        _route_body,
        grid=(T // ROUTE_LANES,),
        in_specs=[pl.BlockSpec((2 * PEER_HEADS, N_KEYS, ROUTE_LANES), lambda i: (0, 0, i))],
        out_specs=(spec, spec, spec, spec),
        out_shape=(out(F32), out(F32), out(BF16), out(BF16)),
        compiler_params=_params(("parallel",)),
        name="peer_route",
    )(st)


PEER_EXPERT_BLOCK = 1024
PEER_I_PER_BLOCK = PEER_EXPERT_BLOCK // N_KEYS


def _peer_body(xn_ref, u_ref, vt_ref, c0_ref, l0_ref, b1_ref, r1_ref, x1_ref, gfin_ref, y_ref, acc, pg):
    e = pl.program_id(1)

    @pl.when(e == 0)
    def _():
        acc[...] = jnp.zeros_like(acc)

    act = lax.dot_general(u_ref[...], xn_ref[...], (((1,), (1,)), ((), ())),
                          preferred_element_type=F32)
    i0 = pl.multiple_of(e * PEER_I_PER_BLOCK, PEER_I_PER_BLOCK)
    for ib in range(PEER_I_PER_BLOCK):
        g = None
        for h in range(PEER_HEADS):
            lrow = l0_ref[h, pl.ds(i0, PEER_I_PER_BLOCK), :][ib:ib + 1, :]
            crow = c0_ref[h, pl.ds(i0, PEER_I_PER_BLOCK), :][ib:ib + 1, :]
            term = jnp.where(r1_ref[h].astype(F32) < lrow, b1_ref[h].astype(F32), 0.0) * crow
            g = term if g is None else g + term
        rows = slice(ib * N_KEYS, (ib + 1) * N_KEYS)
        pg[rows, :] = (g * _gelu(act[rows, :])).astype(BF16)
    acc[...] += jnp.dot(vt_ref[...], pg[...], preferred_element_type=F32)

    @pl.when(e == pl.num_programs(1) - 1)
    def _():
        x2 = x1_ref[...] + acc[...].T
        ms = jnp.mean(x2 * x2, axis=-1, keepdims=True)
        y_ref[...] = (x2 * lax.rsqrt(ms + EPS)) * gfin_ref[...]


def _peer(xn, u_bf, vt_bf, c0, l0, b1, r1, x1, g_final, *, tt):
    T = xn.shape[0]
    tok = pl.BlockSpec((tt, D_MODEL), lambda t, e: (t, 0))
    rt = pl.BlockSpec((PEER_HEADS, N_KEYS, tt), lambda t, e: (0, 0, t))
    return pl.pallas_call(
        _peer_body,
        grid=(T // tt, N_EXPERTS // PEER_EXPERT_BLOCK),
        in_specs=[tok, pl.BlockSpec((PEER_EXPERT_BLOCK, D_MODEL), lambda t, e: (e, 0)),
                  pl.BlockSpec((D_MODEL, PEER_EXPERT_BLOCK), lambda t, e: (0, e)),
                  rt, rt, rt, rt, tok, _const_spec((1, D_MODEL))],
        out_specs=tok,
        out_shape=jax.ShapeDtypeStruct((T, D_MODEL), F32),
        scratch_shapes=[pltpu.VMEM((D_MODEL, tt), F32), pltpu.VMEM((PEER_EXPERT_BLOCK, tt), BF16)],
        compiler_params=_params(("parallel", "arbitrary")),
        name="peer_dense",
    )(xn, u_bf, vt_bf, c0, l0, b1, r1, x1, g_final.reshape(1, D_MODEL))


def _pack_w_in(w_in):
    o_k = D_MODEL + Q_WIDTH
    o_v = o_k + Q_WIDTH
    o_g = o_v + Q_WIDTH
    parts = [w_in[:, :o_k]]
    for gi in range(N_GROUPS):
        parts.append(w_in[:, o_k + gi * HEAD_LANES:o_k + (gi + 1) * HEAD_LANES])
        parts.append(w_in[:, o_v + gi * HEAD_LANES:o_v + (gi + 1) * HEAD_LANES])
    parts.append(w_in[:, o_g:])
    return jnp.concatenate(parts, axis=1).astype(BF16)


def _layer(x, h0_re, h0_im, caches, wts, *, nb, ts, ssm_ts, peer_tt):
    B, S, _ = x.shape
    u_t, q, kv0, kv1, kv2, ga, gb = _in_proj(x, wts["g_mix"], wts["w_in"], nb=nb, ts=ts,
                                             q_dtype=BF16 if caches is None else F32)
    kvs = (kv0, kv1, kv2)
    hs_t, hre, him = _ssm(u_t, h0_re, h0_im, *wts["ssm"], ts=ssm_ts)
    if caches is None:
        outs = [_attn_prompt(q, kvs[gi], gi) for gi in range(N_GROUPS)]
        attn = _combine([o for o, _ in outs], [l for _, l in outs], tm=min(1024, B * S))
        attn = attn.reshape(B, S, HEAD_LANES)
    else:
        attn = _attn_sample(q, kvs, caches)
    x1, xn, st = _merge(hs_t, attn, ga, gb, x, wts["w_glu_a"], wts["w_glu_b"], wts["w_attn_proj"],
                        wts["w_out"], wts["g_ffn"], wts["w_qp"], wts["sub_keys"], nb=nb, ts=ts)
    c0, l0, b1, r1 = _route(st)
    y = _peer(xn, wts["u_tab"], wts["v_tab_t"], c0, l0, b1, r1, x1, wts["g_final"], tt=peer_tt)
    return y.reshape(B, S, D_MODEL), hre, him, kvs


def kernel(x_prompt, x_sample, state_ssm_re, state_ssm_im, cache_kv_w128, cache_kv_w512, cache_kv_w2048, g_mix, w_in, lam_re, lam_im, log_dt, b_re, b_im, c_re, c_im, d_skip, w_glu_a, w_glu_b, w_attn_proj, w_out, g_ffn, w_qp, sub_keys, u_tab, v_tab, g_final):
    assert w_in.shape[0] == 1, "single-layer model"
    wts = {
        "g_mix": g_mix[0],
        "w_in": _pack_w_in(w_in[0]),
        "ssm": _ssm_weights(lam_re[0], lam_im[0], log_dt[0], b_re[0], b_im[0], c_re[0], c_im[0], d_skip[0]),
        "w_glu_a": w_glu_a[0].astype(BF16),
        "w_glu_b": w_glu_b[0].astype(BF16),
        "w_attn_proj": w_attn_proj[0].astype(BF16),
        "w_out": w_out[0].astype(BF16),
        "g_ffn": g_ffn[0],
        "w_qp": w_qp[0].astype(BF16),
        "sub_keys": sub_keys[0].reshape(2 * PEER_HEADS, N_KEYS, PEER_HALF).astype(BF16),
        "u_tab": u_tab[0].astype(BF16),
        "v_tab_t": v_tab[0].astype(BF16).T,
        "g_final": g_final,
    }
    Bp, Sp, _ = x_prompt.shape
    Bs, Ss, _ = x_sample.shape
    assert Ss == SAMPLE_T
    zeros = jnp.zeros((Bp, SSM_LANES), F32)
    yp, hre_p, him_p, kv_p = _layer(x_prompt, zeros, zeros, None, wts, nb=1, ts=256, ssm_ts=32, peer_tt=512)
    caches = (cache_kv_w128[0].reshape(Bs, WINDOWS[0], KV_WIDTH),
              cache_kv_w512[0].reshape(Bs, WINDOWS[1], KV_WIDTH),
              cache_kv_w2048[0].reshape(Bs, WINDOWS[2], KV_WIDTH))
    ys, hre_s, him_s, kv_s = _layer(x_sample, state_ssm_re[0].reshape(Bs, SSM_LANES),
                                    state_ssm_im[0].reshape(Bs, SSM_LANES), caches, wts,
                                    nb=16, ts=SAMPLE_T, ssm_ts=SAMPLE_T, peer_tt=512)

    state = lambda h: h.reshape(1, -1, SSM_GROUPS, SSM_STATE)
    kv_rows = lambda kv, w: kv[:, kv.shape[1] - min(w, kv.shape[1]):].reshape(1, kv.shape[0], -1, 2, HEADS, HEAD_DIM)
    return (yp, ys, state(hre_p), state(him_p),
            kv_rows(kv_p[0], WINDOWS[0]), kv_rows(kv_p[1], WINDOWS[1]), kv_rows(kv_p[2], WINDOWS[2]),
            state(hre_s), state(him_s),
            kv_rows(kv_s[0], SAMPLE_T), kv_rows(kv_s[1], SAMPLE_T), kv_rows(kv_s[2], SAMPLE_T))
```

```python
import functools
import math

import jax
import jax.numpy as jnp
from jax import lax
from jax.experimental import pallas as pl
from jax.experimental.pallas import tpu as pltpu

F32 = jnp.float32
BF16 = jnp.bfloat16

D_MODEL = 1024
SSM_CH = 16
SSM_GROUPS = D_MODEL // SSM_CH
SSM_STATE = 64
SSM_LANES = SSM_GROUPS * SSM_STATE
HEAD_DIM = 64
HEADS = 8
HEAD_LANES = HEADS * HEAD_DIM
N_GROUPS = 3
WINDOWS = (128, 512, 2048)
DILATIONS = (1, 4, 16)
N_STEPS = 128
Q_WIDTH = N_GROUPS * HEAD_LANES
KV_WIDTH = 2 * HEAD_LANES
PROJ_WIDTH = D_MODEL + 3 * Q_WIDTH + 2 * D_MODEL
PEER_HEADS = 8
PEER_TOPK = 16
N_KEYS = 128
N_EXPERTS = N_KEYS * N_KEYS
PEER_HALF = 128
EPS = 1e-6
NEG_INF = float("-inf")

SSM_BATCH_TILE = 16
SSM_BLOCK_GROUPS = 8
SSM_N_BLOCKS = SSM_GROUPS // SSM_BLOCK_GROUPS
SSM_BLOCK_STATE = SSM_BLOCK_GROUPS * SSM_STATE
SCAN_LANES = 512

VMEM_LIMIT_BYTES = 56 * 1024 * 1024
BF16_ROWS = 16


def _params(semantics):
    return pltpu.CompilerParams(dimension_semantics=semantics, vmem_limit_bytes=VMEM_LIMIT_BYTES)


def _const_spec(shape):
    zeros = (0,) * len(shape)
    return pl.BlockSpec(shape, lambda *_: zeros)


def _gelu(x):
    return jax.nn.gelu(x, approximate=True)


_SEG_U = (0, D_MODEL)
_SEG_Q = (D_MODEL, D_MODEL + Q_WIDTH)
_SEG_KV0 = D_MODEL + Q_WIDTH
_SEG_GA = _SEG_KV0 + N_GROUPS * KV_WIDTH
_SEG_GB = _SEG_GA + D_MODEL


LANE_SLABS = D_MODEL // 128


def _in_proj_body(x_ref, g_ref, w_ref, u_ref, *refs, nb, ts, dilated):
    x = x_ref[...].reshape(nb * ts, D_MODEL)
    ms = jnp.mean(x * x, axis=-1, keepdims=True)
    xn = (x * lax.rsqrt(ms + EPS)) * g_ref[...]
    xb = xn.astype(BF16)
    scale = HEAD_DIM ** -0.5

    def seg(lhs, a, b):
        return jnp.dot(lhs, w_ref[:, a:b], preferred_element_type=F32)

    u = seg(xb, *_SEG_U)
    for k in range(nb):
        u_ref[:, k * D_MODEL:(k + 1) * D_MODEL] = u[k * ts:(k + 1) * ts]
    kv_cols = lambda gi: (_SEG_KV0 + gi * KV_WIDTH, _SEG_KV0 + (gi + 1) * KV_WIDTH)
    q_cols = lambda gi: (D_MODEL + gi * HEAD_LANES, D_MODEL + (gi + 1) * HEAD_LANES)

    if not dilated:
        q_ref, kv_refs, (ga_ref, gb_ref) = refs[0], refs[1:4], refs[4:6]
        q_ref[...] = (seg(xb, *_SEG_Q) * scale).reshape(nb, ts, Q_WIDTH)
        for gi in range(N_GROUPS):
            kv_refs[gi][...] = seg(xb, *kv_cols(gi)).reshape(nb, ts, KV_WIDTH)
    else:
        qd_refs, kvd_refs, kv_refs, (ga_ref, gb_ref), (xs, ys) = refs[0:3], refs[3:6], refs[6:9], refs[9:11], refs[11:13]
        for c in range(LANE_SLABS):
            xs[c] = xn[:, c * 128:(c + 1) * 128]
        for gi in range(N_GROUPS):
            d = DILATIONS[gi]
            n = ts // d
            if d == 1:
                lhs = xb
            else:
                lhs = jnp.concatenate(
                    [jnp.concatenate([xs[c, pl.ds(r, n, stride=d), :] for c in range(LANE_SLABS)], axis=1)
                     for r in range(d)], axis=0).astype(BF16)
            qd_refs[gi][0] = (seg(lhs, *q_cols(gi)) * scale).reshape(d, n, HEAD_LANES).astype(BF16)
            kv = seg(lhs, *kv_cols(gi))
            kvd_refs[gi][0] = kv.reshape(d, n, KV_WIDTH).astype(BF16)
            if d == 1:
                kv_refs[gi][0] = kv
            else:
                for r in range(d):
                    for c in range(LANE_SLABS):
                        ys[c, pl.ds(r, n, stride=d), :] = kv[r * n:(r + 1) * n, c * 128:(c + 1) * 128]
                kv_refs[gi][0] = jnp.concatenate([ys[c] for c in range(LANE_SLABS)], axis=1)
    ga_ref[...] = seg(xb, _SEG_GA, _SEG_GB).reshape(nb, ts, D_MODEL)
    gb_ref[...] = seg(xb, _SEG_GB, PROJ_WIDTH).reshape(nb, ts, D_MODEL)


def _in_proj(x, g_mix, w_perm, *, nb, ts, dilated):
    B, S, _ = x.shape
    assert not dilated or nb == 1
    row = lambda w: pl.BlockSpec((nb, ts, w), lambda b, s: (b, s, 0))
    f32_rows = lambda w: jax.ShapeDtypeStruct((B, S, w), F32)
    natural = [(row(KV_WIDTH), f32_rows(KV_WIDTH))] * N_GROUPS + [(row(D_MODEL), f32_rows(D_MODEL))] * 2
    if dilated:
        def regrouped(w):
            return [(pl.BlockSpec((1, d, ts // d, w), lambda b, s: (b, 0, s, 0)),
                     jax.ShapeDtypeStruct((B, d, S // d, w), BF16)) for d in DILATIONS]
        outs = regrouped(HEAD_LANES) + regrouped(KV_WIDTH) + natural
        scratch = [pltpu.VMEM((LANE_SLABS, ts, 128), F32), pltpu.VMEM((LANE_SLABS, ts, 128), F32)]
    else:
        outs = [(row(Q_WIDTH), f32_rows(Q_WIDTH))] + natural
        scratch = []
    outs = [(pl.BlockSpec((ts, nb * D_MODEL), lambda b, s: (s, b)), jax.ShapeDtypeStruct((S, B * D_MODEL), F32))] + outs
    return pl.pallas_call(
        functools.partial(_in_proj_body, nb=nb, ts=ts, dilated=dilated),
        grid=(B // nb, S // ts),
        in_specs=[row(D_MODEL), _const_spec((1, D_MODEL)),
                  pl.BlockSpec((D_MODEL, PROJ_WIDTH), lambda b, s: (0, 0), pipeline_mode=pl.Buffered(1))],
        out_specs=tuple(spec for spec, _ in outs),
        out_shape=tuple(shape for _, shape in outs),
        scratch_shapes=scratch,
        compiler_params=_params(("parallel", "parallel")),
        name="in_proj",
    )(x, g_mix.reshape(1, D_MODEL), w_perm)


def _ssm_body(u_ref, h0re_ref, h0im_ref, wx_ref, cy_ref, are_ref, aim_ref, dsk_ref,
              hs_ref, hre_ref, him_ref, xre, xim, st_re, st_im, *, ts):
    si = pl.program_id(1)
    rows = ts * SSM_BATCH_TILE

    @pl.when(si == 0)
    def _():
        st_re[...] = h0re_ref[...]
        st_im[...] = h0im_ref[...]

    u = u_ref[...].reshape(rows, D_MODEL)
    ub = u.astype(BF16)
    for j in range(SSM_N_BLOCKS):
        r = jnp.dot(ub[:, j * 128:(j + 1) * 128], wx_ref[j], preferred_element_type=F32)
        cols = slice(j * SSM_BLOCK_STATE, (j + 1) * SSM_BLOCK_STATE)
        xre[:, cols] = r[:, :SSM_BLOCK_STATE]
        xim[:, cols] = r[:, SSM_BLOCK_STATE:]

    for c in range(SSM_LANES // SCAN_LANES):
        cols = slice(c * SCAN_LANES, (c + 1) * SCAN_LANES)
        a_re = jnp.broadcast_to(are_ref[:, cols], (SSM_BATCH_TILE, SCAN_LANES))
        a_im = jnp.broadcast_to(aim_ref[:, cols], (SSM_BATCH_TILE, SCAN_LANES))

        def step(t, carry, cols=cols, a_re=a_re, a_im=a_im):
            h_re, h_im = carry
            r0 = pl.multiple_of(t * SSM_BATCH_TILE, SSM_BATCH_TILE)
            n_re = a_re * h_re - a_im * h_im + xre[pl.ds(r0, SSM_BATCH_TILE), cols]
            n_im = a_re * h_im + a_im * h_re + xim[pl.ds(r0, SSM_BATCH_TILE), cols]
            xre[pl.ds(r0, SSM_BATCH_TILE), cols] = n_re
            xim[pl.ds(r0, SSM_BATCH_TILE), cols] = n_im
            return n_re, n_im

        h_re, h_im = lax.fori_loop(0, ts, step, (st_re[:, cols], st_im[:, cols]), unroll=min(ts, 8))
        st_re[:, cols] = h_re
        st_im[:, cols] = h_im

    for j in range(SSM_N_BLOCKS):
        cols = slice(j * SSM_BLOCK_STATE, (j + 1) * SSM_BLOCK_STATE)
        h = jnp.concatenate([xre[:, cols], xim[:, cols]], axis=1).astype(BF16)
        y = jnp.dot(h, cy_ref[j], preferred_element_type=F32)
        ch = slice(j * 128, (j + 1) * 128)
        y = y + dsk_ref[:, ch] * u[:, ch]
        hs_ref[:, :, ch] = _gelu(y).reshape(ts, SSM_BATCH_TILE, 128)

    @pl.when(si == pl.num_programs(1) - 1)
    def _():
        hre_ref[...] = st_re[...]
        him_ref[...] = st_im[...]


def _ssm(u_t, h0_re, h0_im, wx, cy, a_re, a_im, dsk, *, ts):
    S = u_t.shape[0]
    B = u_t.shape[1] // D_MODEL
    nbt = B // SSM_BATCH_TILE
    u4 = u_t.reshape(S, nbt, SSM_BATCH_TILE, D_MODEL)
    tile = pl.BlockSpec((ts, None, SSM_BATCH_TILE, D_MODEL), lambda b, s: (s, b, 0, 0))
    state = pl.BlockSpec((SSM_BATCH_TILE, SSM_LANES), lambda b, s: (b, 0))
    rows = ts * SSM_BATCH_TILE
    hs, hre, him = pl.pallas_call(
        functools.partial(_ssm_body, ts=ts),
        grid=(nbt, S // ts),
        in_specs=[tile, state, state,
                  _const_spec((SSM_N_BLOCKS, 128, 2 * SSM_BLOCK_STATE)),
                  _const_spec((SSM_N_BLOCKS, 2 * SSM_BLOCK_STATE, 128)),
                  _const_spec((1, SSM_LANES)), _const_spec((1, SSM_LANES)), _const_spec((1, D_MODEL))],
        out_specs=(tile, state, state),
        out_shape=(jax.ShapeDtypeStruct(u4.shape, F32),
                   jax.ShapeDtypeStruct((B, SSM_LANES), F32), jax.ShapeDtypeStruct((B, SSM_LANES), F32)),
        scratch_shapes=[pltpu.VMEM((rows, SSM_LANES), F32), pltpu.VMEM((rows, SSM_LANES), F32),
                        pltpu.VMEM((SSM_BATCH_TILE, SSM_LANES), F32), pltpu.VMEM((SSM_BATCH_TILE, SSM_LANES), F32)],
        compiler_params=_params(("parallel", "arbitrary")),
        name="ssm",
    )(u4, h0_re, h0_im, wx, cy, a_re, a_im, dsk)
    return hs.reshape(S, B * D_MODEL), hre, him


def _ssm_weights(lam_re, lam_im, log_dt, b_re, b_im, c_re, c_im, d_skip):
    dt = jnp.exp(log_dt)[:, None]
    mag = jnp.exp(lam_re * dt)
    ang = lam_im * dt
    ab_re, ab_im = mag * jnp.cos(ang), mag * jnp.sin(ang)
    den = lam_re * lam_re + lam_im * lam_im
    f_re = ((ab_re - 1.0) * lam_re + ab_im * lam_im) / den
    f_im = (ab_im * lam_re - (ab_re - 1.0) * lam_im) / den
    bb_re = f_re[..., None] * b_re - f_im[..., None] * b_im
    bb_im = f_re[..., None] * b_im + f_im[..., None] * b_re
    eye = jnp.eye(SSM_BLOCK_GROUPS, dtype=F32)

    def in_block(bb):
        bb = bb.reshape(SSM_N_BLOCKS, SSM_BLOCK_GROUPS, SSM_STATE, SSM_CH)
        return jnp.einsum('jgnc,gh->jgchn', bb, eye).reshape(SSM_N_BLOCKS, 128, SSM_BLOCK_STATE)

    def out_block(cc):
        cc = cc.reshape(SSM_N_BLOCKS, SSM_BLOCK_GROUPS, SSM_CH, SSM_STATE)
        return jnp.einsum('jgcn,gh->jgnhc', cc, eye).reshape(SSM_N_BLOCKS, SSM_BLOCK_STATE, 128)

    wx = jnp.concatenate([in_block(bb_re), in_block(bb_im)], axis=2).astype(BF16)
    cy = jnp.concatenate([out_block(c_re), out_block(-c_im)], axis=1).astype(BF16)
    return (wx, cy, ab_re.reshape(1, SSM_LANES), ab_im.reshape(1, SSM_LANES), d_skip.reshape(1, D_MODEL))


def _head_masks():
    lane = lax.broadcasted_iota(jnp.int32, (1, 128), 1)
    return [(lane // HEAD_DIM) == hh for hh in range(2)]


def _attn_body(q_ref, kc_ref, kp_ref, o_ref, lse_ref):
    n = pl.program_id(2)
    q = q_ref[0, 0]
    kvc = kc_ref[0, 0]
    kvp = kp_ref[0, 0]
    kcat = jnp.concatenate([kvp[:, :HEAD_LANES], kvc[:, :HEAD_LANES]], axis=0)
    vcat = jnp.concatenate([kvp[:, HEAD_LANES:], kvc[:, HEAD_LANES:]], axis=0)
    qi = lax.broadcasted_iota(jnp.int32, (N_STEPS, 2 * N_STEPS), 0)
    ki = lax.broadcasted_iota(jnp.int32, (N_STEPS, 2 * N_STEPS), 1)
    first_key = jnp.where(n > 0, 0, N_STEPS)
    mask = (ki >= jnp.maximum(qi, first_key)) & (ki <= qi + N_STEPS)
    hm = _head_masks()
    lane = lax.broadcasted_iota(jnp.int32, (1, 128), 1)
    lse_tile = jnp.zeros((N_STEPS, 128), F32)
    for hp in range(HEADS // 2):
        ch = slice(hp * 128, (hp + 1) * 128)
        qp, kp, vp = q[:, ch], kcat[:, ch], vcat[:, ch]
        outs = []
        for hh in range(2):
            qm = jnp.where(hm[hh], qp, jnp.zeros_like(qp))
            s = lax.dot_general(qm, kp, (((1,), (1,)), ((), ())), preferred_element_type=F32)
            s = jnp.where(mask, s, NEG_INF)
            m = jnp.max(s, axis=-1, keepdims=True)
            p = jnp.exp(s - m)
            den = jnp.sum(p, axis=-1, keepdims=True)
            outs.append(jnp.dot((p / den).astype(BF16), vp, preferred_element_type=F32))
            lse = m + jnp.log(den)
            lse_tile = jnp.where((lane // 16) == (2 * hp + hh), lse, lse_tile)
        o_ref[0, 0, :, ch] = jnp.where(hm[0], outs[0], outs[1])
    lse_ref[0, 0] = lse_tile


def _attn_prompt(q, kv, gi):
    B, d, L, _ = q.shape
    nblk = L // N_STEPS
    blk = lambda w, prev: pl.BlockSpec((1, 1, N_STEPS, w),
                                       (lambda b, r, n: (b, r, jnp.maximum(n - 1, 0), 0)) if prev
                                       else (lambda b, r, n: (b, r, n, 0)))
    return pl.pallas_call(
        _attn_body,
        grid=(B, d, nblk),
        in_specs=[blk(HEAD_LANES, False), blk(KV_WIDTH, False), blk(KV_WIDTH, True)],
        out_specs=(blk(HEAD_LANES, False), blk(128, False)),
        out_shape=(jax.ShapeDtypeStruct((B, d, L, HEAD_LANES), F32),
                   jax.ShapeDtypeStruct((B, d, L, 128), F32)),
        compiler_params=_params(("parallel", "parallel", "arbitrary")),
        name=f"attn_prompt_g{gi}",
    )(q, kv, kv)


def _combine_body(o0_ref, o1_ref, o2_ref, l0_ref, l1_ref, l2_ref, ex_ref, attn_ref, oslab, lslab, *, tm):
    def natural(ref, gi, slab):
        d = DILATIONS[gi]
        if d == 1:
            return ref[0, 0]
        n = tm // d
        nslab = ref.shape[-1] // 128
        for r in range(d):
            for c in range(nslab):
                slab[c, pl.ds(r, n, stride=d), :] = ref[0, r, :, c * 128:(c + 1) * 128]
        return jnp.concatenate([slab[c] for c in range(nslab)], axis=1)

    o_refs, l_refs = (o0_ref, o1_ref, o2_ref), (l0_ref, l1_ref, l2_ref)
    ls = [natural(l_refs[gi], gi, lslab) for gi in range(N_GROUPS)]
    m = jnp.maximum(jnp.maximum(ls[0], ls[1]), ls[2])
    es = [jnp.exp(l - m) for l in ls]
    den = es[0] + es[1] + es[2]
    acc = None
    for gi in range(N_GROUPS):
        w = es[gi] / den
        w_hi = w.astype(BF16)
        w_lo = (w - w_hi.astype(F32)).astype(BF16)
        wx = (jnp.dot(w_hi, ex_ref[...], preferred_element_type=F32)
              + jnp.dot(w_lo, ex_ref[...], preferred_element_type=F32))
        term = wx * natural(o_refs[gi], gi, oslab)
        acc = term if acc is None else acc + term
    attn_ref[0] = acc.astype(BF16)


def _head_expand_matrix():
    src = jnp.arange(128)[:, None]
    dst = jnp.arange(HEAD_LANES)[None, :]
    return (src == (dst // HEAD_DIM) * 16).astype(BF16)


def _combine(os_, lses, *, tm):
    B = os_[0].shape[0]
    S = os_[0].shape[1] * os_[0].shape[2]
    blk = lambda d, w: pl.BlockSpec((1, d, tm // d, w), lambda b, s: (b, 0, s, 0))
    return pl.pallas_call(
        functools.partial(_combine_body, tm=tm),
        grid=(B, S // tm),
        in_specs=[blk(d, HEAD_LANES) for d in DILATIONS] + [blk(d, 128) for d in DILATIONS]
                 + [_const_spec((128, HEAD_LANES))],
        out_specs=pl.BlockSpec((1, tm, HEAD_LANES), lambda b, s: (b, s, 0)),
        out_shape=jax.ShapeDtypeStruct((B, S, HEAD_LANES), BF16),
        scratch_shapes=[pltpu.VMEM((HEAD_LANES // 128, tm, 128), F32), pltpu.VMEM((1, tm, 128), F32)],
        compiler_params=_params(("parallel", "parallel")),
        name="attn_combine",
    )(*os_, *lses, _head_expand_matrix())


SAMPLE_T = 8
_NT = (((1,), (1,)), ((), ()))


def _attn_sample_body(q_ref, n0_ref, n1_ref, n2_ref, c0_ref, c1_ref, c2_ref, attn_ref):
    q = q_ref[0]
    new_refs = (n0_ref, n1_ref, n2_ref)
    cache_refs = (c0_ref, c1_ref, c2_ref)
    hm = _head_masks()
    rows = 2 * SAMPLE_T

    def stride_mask(shape, d, lower):
        t = lax.broadcasted_iota(jnp.int32, shape, 0) % SAMPLE_T
        p = lax.broadcasted_iota(jnp.int32, shape, 1)
        back = (t - p) if lower else (p - t)
        return (back >= 0) & ((back & (d - 1)) == 0)

    cache_ok = [stride_mask((rows, WINDOWS[gi]), DILATIONS[gi], False) for gi in range(N_GROUPS)]
    new_ok = [stride_mask((rows, SAMPLE_T), DILATIONS[gi], True) for gi in range(N_GROUPS)]

    for hp in range(HEADS // 2):
        ch = slice(hp * 128, (hp + 1) * 128)
        scores = []
        for gi in range(N_GROUPS):
            qp = q[:, gi * HEAD_LANES + hp * 128:gi * HEAD_LANES + (hp + 1) * 128]
            lhs = jnp.concatenate([jnp.where(hm[0], qp, 0.0), jnp.where(hm[1], qp, 0.0)], axis=0).astype(BF16)
            kt = cache_refs[gi][0, 0, 2 * hp:2 * hp + 2].reshape(128, WINDOWS[gi]).astype(BF16)
            s = jnp.dot(lhs, kt, preferred_element_type=F32)
            kn = new_refs[gi][0, :, ch].astype(BF16)
            sn = lax.dot_general(lhs, kn, _NT, preferred_element_type=F32)
            scores.append((jnp.where(cache_ok[gi], s, NEG_INF), jnp.where(new_ok[gi], sn, NEG_INF)))
        m = functools.reduce(jnp.maximum, [jnp.max(x, axis=-1, keepdims=True) for pair in scores for x in pair])
        den = jnp.zeros((rows, 1), F32)
        o = jnp.zeros((rows, 128), F32)
        for gi in range(N_GROUPS):
            p = jnp.exp(scores[gi][0] - m)
            pn = jnp.exp(scores[gi][1] - m)
            den = den + jnp.sum(p, axis=-1, keepdims=True) + jnp.sum(pn, axis=-1, keepdims=True)
            vt = cache_refs[gi][0, 1, 2 * hp:2 * hp + 2].reshape(128, WINDOWS[gi]).astype(BF16)
            vn = new_refs[gi][0, :, HEAD_LANES + hp * 128:HEAD_LANES + (hp + 1) * 128].astype(BF16)
            o = o + lax.dot_general(p.astype(BF16), vt, _NT, preferred_element_type=F32)
            o = o + jnp.dot(pn.astype(BF16), vn, preferred_element_type=F32)
        o = o / den
        attn_ref[0, :, ch] = jnp.where(hm[0], o[:SAMPLE_T], o[SAMPLE_T:])


def _attn_sample(q, kv_new, caches_t):
    B = q.shape[0]
    specs = []
    for gi, c in enumerate(caches_t):
        assert c.shape[1:] == (2, HEADS, HEAD_DIM, WINDOWS[gi]), "decode caches must hold a full window"
        specs.append(pl.BlockSpec((1,) + c.shape[1:], lambda b: (b, 0, 0, 0, 0)))
    new_spec = pl.BlockSpec((1, SAMPLE_T, KV_WIDTH), lambda b: (b, 0, 0))
    return pl.pallas_call(
        _attn_sample_body,
        grid=(B,),
        in_specs=[pl.BlockSpec((1, SAMPLE_T, Q_WIDTH), lambda b: (b, 0, 0)), new_spec, new_spec, new_spec, *specs],
        out_specs=pl.BlockSpec((1, SAMPLE_T, HEAD_LANES), lambda b: (b, 0, 0)),
        out_shape=jax.ShapeDtypeStruct((B, SAMPLE_T, HEAD_LANES), F32),
        compiler_params=_params(("parallel",)),
        name="attn_sample",
    )(q, *kv_new, *caches_t)


def _merge_body(hs_ref, attn_ref, ga_ref, gb_ref, x_ref, wa_ref, wb_ref, wp_ref, wo_ref, gf_ref, wq_ref, sk_ref,
                x1_ref, xn_ref, st_ref, *, nb, ts):
    rows = nb * ts
    hs = jnp.concatenate([hs_ref[:, k * D_MODEL:(k + 1) * D_MODEL] for k in range(nb)], axis=0).astype(BF16)
    dot = lambda a, w_ref: jnp.dot(a, w_ref[...], preferred_element_type=F32)
    branch_a = dot(hs, wa_ref) * jax.nn.sigmoid(dot(hs, wb_ref))
    branch_b = dot(attn_ref[...].reshape(rows, HEAD_LANES).astype(BF16), wp_ref)
    ga = ga_ref[...].reshape(rows, D_MODEL)
    gb = gb_ref[...].reshape(rows, D_MODEL)
    mix = jax.nn.sigmoid(ga) * branch_a + jax.nn.sigmoid(gb) * branch_b
    x1 = x_ref[...].reshape(rows, D_MODEL) + dot(mix.astype(BF16), wo_ref)
    x1_ref[...] = x1
    ms = jnp.mean(x1 * x1, axis=-1, keepdims=True)
    xn = ((x1 * lax.rsqrt(ms + EPS)) * gf_ref[...]).astype(BF16)
    xn_ref[...] = xn
    qp = dot(xn, wq_ref).astype(BF16)
    for hp in range(2 * PEER_HEADS):
        st_ref[hp] = lax.dot_general(sk_ref[hp], qp[:, hp * PEER_HALF:(hp + 1) * PEER_HALF],
                                     (((1,), (1,)), ((), ())), preferred_element_type=F32)


def _merge(hs_t, attn, ga, gb, x, wa, wb, wp, wo, g_ffn, wq, sk, *, nb, ts):
    B, S, _ = x.shape
    rows = nb * ts
    T = B * S
    nsb = S // ts
    row = lambda w: pl.BlockSpec((nb, ts, w), lambda b, s: (b, s, 0))
    flat = lambda w: pl.BlockSpec((rows, w), lambda b, s: (b * nsb + s, 0))
    return pl.pallas_call(
        functools.partial(_merge_body, nb=nb, ts=ts),
        grid=(B // nb, nsb),
        in_specs=[pl.BlockSpec((ts, nb * D_MODEL), lambda b, s: (s, b)), row(HEAD_LANES), row(D_MODEL),
                  row(D_MODEL), row(D_MODEL),
                  _const_spec((D_MODEL, D_MODEL)), _const_spec((D_MODEL, D_MODEL)),
                  _const_spec((HEAD_LANES, D_MODEL)), _const_spec((D_MODEL, D_MODEL)),
                  _const_spec((1, D_MODEL)), _const_spec((D_MODEL, 2 * PEER_HEADS * PEER_HALF)),
                  _const_spec((2 * PEER_HEADS, N_KEYS, PEER_HALF))],
        out_specs=(flat(D_MODEL), flat(D_MODEL),
                   pl.BlockSpec((2 * PEER_HEADS, N_KEYS, rows), lambda b, s: (0, 0, b * nsb + s))),
        out_shape=(jax.ShapeDtypeStruct((T, D_MODEL), F32), jax.ShapeDtypeStruct((T, D_MODEL), BF16),
                   jax.ShapeDtypeStruct((2 * PEER_HEADS, N_KEYS, T), F32)),
        compiler_params=_params(("parallel", "parallel")),
        name="merge",
    )(hs_t, attn, ga, gb, x, wa, wb, wp, wo, g_ffn.reshape(1, D_MODEL), wq, sk)


ROUTE_LANES = 128
_PAIR_SLABS = ((0, 0, 0, 1, 8), (0, 0, 8, 1, 8), (1, 0, 0, 1, 8), (2, 0, 0, 1, 5), (3, 0, 0, 1, 4),
               (4, 0, 0, 1, 3), (5, 0, 0, 1, 2), (6, 0, 0, 1, 2), (7, 0, 0, 1, 2), (8, 1, 0, 0, 8))


def _top16(s, break_ties):
    idx = lax.broadcasted_iota(jnp.int32, s.shape, 0).astype(F32)
    cur = s
    rank = jnp.full(s.shape, float(PEER_TOPK), F32)
    vals = []
    for r in range(PEER_TOPK):
        m = jnp.max(cur, axis=0, keepdims=True)
        sel = cur == m
        if break_ties:
            sel = idx == jnp.min(jnp.where(sel, idx, float(N_KEYS)), axis=0, keepdims=True)
        rank = jnp.where(sel, float(r), rank)
        cur = jnp.where(sel, NEG_INF, cur)
        vals.append(m)
    n_ranked = jnp.sum(jnp.where(rank < float(PEER_TOPK), 1.0, 0.0), axis=0, keepdims=True)
    return rank, vals, n_ranked


def _route_head(s0, s1, break_ties):
    sub = lax.broadcasted_iota(jnp.int32, (8, ROUTE_LANES), 0)
    rank0, v0, n0 = _top16(s0, break_ties)
    rank1, v1, n1 = _top16(s1, break_ties)
    v0b = jnp.concatenate(v0[8:], axis=0)
    v1a, v1b = jnp.concatenate(v1[:8], axis=0), jnp.concatenate(v1[8:], axis=0)

    cands, cidx = [], []
    for (r0, r0s, r1, r1s, nrow) in _PAIR_SLABS:
        a = (v0b if r0 == 8 else v0[r0])
        b = (v1[0] if r1s == 0 else (v1a if r1 == 0 else v1b))
        c = a + b
        if nrow < 8:
            c = jnp.where(sub < nrow, c, NEG_INF)
        cands.append(c)
        cidx.append(((r0 + r0s * sub) * PEER_TOPK + (r1 + r1s * sub)).astype(F32))

    cur = list(cands)
    picked = [jnp.zeros((8, ROUTE_LANES), F32) for _ in cands]
    big = float(PEER_TOPK * PEER_TOPK)
    for _ in range(PEER_TOPK):
        m = jnp.max(functools.reduce(jnp.maximum, cur), axis=0, keepdims=True)
        sels = [c == m for c in cur]
        if break_ties:
            first = functools.reduce(jnp.minimum, [jnp.where(s, ci, big) for s, ci in zip(sels, cidx)])
            first = jnp.min(first, axis=0, keepdims=True)
            sels = [ci == first for ci in cidx]
        for k, sel in enumerate(sels):
            picked[k] = jnp.where(sel, 1.0, picked[k])
            cur[k] = jnp.where(sel, NEG_INF, cur[k])

    total = functools.reduce(lambda x, y: x + y, picked)
    n_pairs = jnp.sum(total, axis=0, keepdims=True)
    top = v0[0] + v1[0]
    z = functools.reduce(lambda x, y: x + y,
                         [jnp.sum(pk * jnp.exp(c - top), axis=0, keepdims=True) for pk, c in zip(picked, cands)])
    counts = [jnp.sum(picked[0] + picked[1], axis=0, keepdims=True)]
    counts += [jnp.sum(picked[k], axis=0, keepdims=True) for k in range(2, 9)]
    counts += [picked[9][i:i + 1, :] for i in range(8)]
    l0 = jnp.zeros((N_KEYS, ROUTE_LANES), F32)
    for r in range(PEER_TOPK):
        l0 = jnp.where(rank0 == float(r), counts[r], l0)

    k16 = float(PEER_TOPK)
    n_bad = jnp.abs(n0 - k16) + jnp.abs(n1 - k16) + jnp.abs(n_pairs - k16)
    return jnp.exp(s0 - v0[0]) / z, l0, jnp.exp(s1 - v1[0]), rank1, n_bad


ROUTE_HEADS_PER_ITER = 2


def _route_body(st_ref, c0_ref, l0_ref, b1_ref, r1_ref):
    def heads(it, carry):
        def run(break_ties):
            n_bad = None
            for k in range(ROUTE_HEADS_PER_ITER):
                h = it * ROUTE_HEADS_PER_ITER + k
                c0, l0, b1, r1, bad = _route_head(st_ref[2 * h], st_ref[2 * h + 1], break_ties)
                c0_ref[h] = c0
                l0_ref[h] = l0
                b1_ref[h] = b1.astype(BF16)
                r1_ref[h] = r1.astype(BF16)
                n_bad = bad if n_bad is None else n_bad + bad
            return jnp.max(n_bad)

        n_bad = run(False)

        @pl.when(n_bad > 0.0)
        def _():
            run(True)

        return carry

    lax.fori_loop(0, PEER_HEADS // ROUTE_HEADS_PER_ITER, heads, 0)


def _route(st):
    T = st.shape[2]
    out = lambda dt: jax.ShapeDtypeStruct((PEER_HEADS, N_KEYS, T), dt)
    spec = pl.BlockSpec((PEER_HEADS, N_KEYS, ROUTE_LANES), lambda i: (0, 0, i))
    return pl.pallas_call(
        _route_body,
        grid=(T // ROUTE_LANES,),
        in_specs=[pl.BlockSpec((2 * PEER_HEADS, N_KEYS, ROUTE_LANES), lambda i: (0, 0, i))],
        out_specs=(spec, spec, spec, spec),
        out_shape=(out(F32), out(F32), out(BF16), out(BF16)),
        compiler_params=_params(("parallel",)),
        name="peer_route",
    )(st)


PEER_EXPERT_BLOCK = 1024
PEER_I_PER_BLOCK = PEER_EXPERT_BLOCK // N_KEYS


def _peer_body(xn_ref, u_ref, vt_ref, c0_ref, l0_ref, b1_ref, r1_ref, x1_ref, gfin_ref, y_ref, acc, pg):
    e = pl.program_id(1)

    @pl.when(e == 0)
    def _():
        acc[...] = jnp.zeros_like(acc)

    act = lax.dot_general(u_ref[...], xn_ref[...], (((1,), (1,)), ((), ())),
                          preferred_element_type=F32)
    tt = act.shape[1]
    tile = (N_KEYS // BF16_ROWS, BF16_ROWS, tt)
    zero = jnp.zeros(tile, BF16)
    for ib in range(PEER_I_PER_BLOCK):
        g = None
        for h in range(PEER_HEADS):
            lrow = jnp.broadcast_to(l0_ref[h, ib:ib + 1, :], (BF16_ROWS, tt)).astype(BF16)
            crow = jnp.broadcast_to(c0_ref[h, ib:ib + 1, :], (BF16_ROWS, tt)).astype(BF16)
            term = jnp.where(r1_ref[h].reshape(tile) < lrow[None], b1_ref[h].reshape(tile), zero) * crow[None]
            g = term if g is None else g + term
        rows = slice(ib * N_KEYS, (ib + 1) * N_KEYS)
        pg[rows, :] = (g * _gelu(act[rows, :].astype(BF16).reshape(tile))).reshape(N_KEYS, tt)
    acc[...] += jnp.dot(vt_ref[...], pg[...], preferred_element_type=F32)

    @pl.when(e == pl.num_programs(1) - 1)
    def _():
        x2 = x1_ref[...] + acc[...].T
        ms = jnp.mean(x2 * x2, axis=-1, keepdims=True)
        y_ref[...] = (x2 * lax.rsqrt(ms + EPS)) * gfin_ref[...]


def _peer(xn, u_bf, vt_bf, c0, l0, b1, r1, x1, g_final, *, tt):
    T = xn.shape[0]
    tok = pl.BlockSpec((tt, D_MODEL), lambda t, e: (t, 0))
    rt = pl.BlockSpec((PEER_HEADS, N_KEYS, tt), lambda t, e: (0, 0, t))
    rows = pl.BlockSpec((PEER_HEADS, PEER_I_PER_BLOCK, tt), lambda t, e: (0, e, t))
    return pl.pallas_call(
        _peer_body,
        grid=(T // tt, N_EXPERTS // PEER_EXPERT_BLOCK),
        in_specs=[tok, pl.BlockSpec((PEER_EXPERT_BLOCK, D_MODEL), lambda t, e: (e, 0)),
                  pl.BlockSpec((D_MODEL, PEER_EXPERT_BLOCK), lambda t, e: (0, e)),
                  rows, rows, rt, rt, tok, _const_spec((1, D_MODEL))],
        out_specs=tok,
        out_shape=jax.ShapeDtypeStruct((T, D_MODEL), F32),
        scratch_shapes=[pltpu.VMEM((D_MODEL, tt), F32), pltpu.VMEM((PEER_EXPERT_BLOCK, tt), BF16)],
        compiler_params=_params(("parallel", "arbitrary")),
        name="peer_dense",
    )(xn, u_bf, vt_bf, c0, l0, b1, r1, x1, g_final.reshape(1, D_MODEL))


def _pack_w_in(w_in):
    o_k = D_MODEL + Q_WIDTH
    o_v = o_k + Q_WIDTH
    o_g = o_v + Q_WIDTH
    parts = [w_in[:, :o_k]]
    for gi in range(N_GROUPS):
        parts.append(w_in[:, o_k + gi * HEAD_LANES:o_k + (gi + 1) * HEAD_LANES])
        parts.append(w_in[:, o_v + gi * HEAD_LANES:o_v + (gi + 1) * HEAD_LANES])
    parts.append(w_in[:, o_g:])
    return jnp.concatenate(parts, axis=1).astype(BF16)


def _layer(x, h0_re, h0_im, caches, wts, *, nb, ts, ssm_ts, peer_tt):
    B, S, _ = x.shape
    proj = _in_proj(x, wts["g_mix"], wts["w_in"], nb=nb, ts=ts, dilated=caches is None)
    u_t, kvs, (ga, gb) = proj[0], proj[-5:-2], proj[-2:]
    hs_t, hre, him = _ssm(u_t, h0_re, h0_im, *wts["ssm"], ts=ssm_ts)
    if caches is None:
        outs = [_attn_prompt(proj[1 + gi], proj[1 + N_GROUPS + gi], gi) for gi in range(N_GROUPS)]
        attn = _combine([o for o, _ in outs], [l for _, l in outs], tm=ts)
    else:
        attn = _attn_sample(proj[1], kvs, caches)
    x1, xn, st = _merge(hs_t, attn, ga, gb, x, wts["w_glu_a"], wts["w_glu_b"], wts["w_attn_proj"],
                        wts["w_out"], wts["g_ffn"], wts["w_qp"], wts["sub_keys"], nb=nb, ts=ts)
    c0, l0, b1, r1 = _route(st)
    y = _peer(xn, wts["u_tab"], wts["v_tab_t"], c0, l0, b1, r1, x1, wts["g_final"], tt=peer_tt)
    return y.reshape(B, S, D_MODEL), hre, him, kvs


def kernel(x_prompt, x_sample, state_ssm_re, state_ssm_im, cache_kv_w128, cache_kv_w512, cache_kv_w2048, g_mix, w_in, lam_re, lam_im, log_dt, b_re, b_im, c_re, c_im, d_skip, w_glu_a, w_glu_b, w_attn_proj, w_out, g_ffn, w_qp, sub_keys, u_tab, v_tab, g_final):
    assert w_in.shape[0] == 1, "single-layer model"
    wts = {
        "g_mix": g_mix[0],
        "w_in": _pack_w_in(w_in[0]),
        "ssm": _ssm_weights(lam_re[0], lam_im[0], log_dt[0], b_re[0], b_im[0], c_re[0], c_im[0], d_skip[0]),
        "w_glu_a": w_glu_a[0].astype(BF16),
        "w_glu_b": w_glu_b[0].astype(BF16),
        "w_attn_proj": w_attn_proj[0].astype(BF16),
        "w_out": w_out[0].astype(BF16),
        "g_ffn": g_ffn[0],
        "w_qp": w_qp[0].astype(BF16),
        "sub_keys": sub_keys[0].reshape(2 * PEER_HEADS, N_KEYS, PEER_HALF).astype(BF16),
        "u_tab": u_tab[0].astype(BF16),
        "v_tab_t": v_tab[0].astype(BF16).T,
        "g_final": g_final,
    }
    Bp, Sp, _ = x_prompt.shape
    Bs, Ss, _ = x_sample.shape
    assert Ss == SAMPLE_T
    zeros = jnp.zeros((Bp, SSM_LANES), F32)
    yp, hre_p, him_p, kv_p = _layer(x_prompt, zeros, zeros, None, wts, nb=1, ts=256, ssm_ts=32, peer_tt=512)
    caches = tuple(c[0].transpose(0, 2, 3, 4, 1) for c in (cache_kv_w128, cache_kv_w512, cache_kv_w2048))
    ys, hre_s, him_s, kv_s = _layer(x_sample, state_ssm_re[0].reshape(Bs, SSM_LANES),
                                    state_ssm_im[0].reshape(Bs, SSM_LANES), caches, wts,
                                    nb=16, ts=SAMPLE_T, ssm_ts=SAMPLE_T, peer_tt=512)

    state = lambda h: h.reshape(1, -1, SSM_GROUPS, SSM_STATE)
    kv_rows = lambda kv, w: kv[:, kv.shape[1] - min(w, kv.shape[1]):].reshape(1, kv.shape[0], -1, 2, HEADS, HEAD_DIM)
    return (yp, ys, state(hre_p), state(him_p),
            kv_rows(kv_p[0], WINDOWS[0]), kv_rows(kv_p[1], WINDOWS[1]), kv_rows(kv_p[2], WINDOWS[2]),
            state(hre_s), state(him_s),
            kv_rows(kv_s[0], SAMPLE_T), kv_rows(kv_s[1], SAMPLE_T), kv_rows(kv_s[2], SAMPLE_T))
```

```python
import functools
import math

import jax
import jax.numpy as jnp
from jax import lax
from jax.experimental import pallas as pl
from jax.experimental.pallas import tpu as pltpu

F32 = jnp.float32
BF16 = jnp.bfloat16

D_MODEL = 1024
SSM_CH = 16
SSM_GROUPS = D_MODEL // SSM_CH
SSM_STATE = 64
SSM_LANES = SSM_GROUPS * SSM_STATE
HEAD_DIM = 64
HEADS = 8
HEAD_LANES = HEADS * HEAD_DIM
N_GROUPS = 3
WINDOWS = (128, 512, 2048)
DILATIONS = (1, 4, 16)
N_STEPS = 128
Q_WIDTH = N_GROUPS * HEAD_LANES
KV_WIDTH = 2 * HEAD_LANES
PROJ_WIDTH = D_MODEL + 3 * Q_WIDTH + 2 * D_MODEL
PEER_HEADS = 8
PEER_TOPK = 16
N_KEYS = 128
N_EXPERTS = N_KEYS * N_KEYS
PEER_HALF = 128
EPS = 1e-6
NEG_INF = float("-inf")

SSM_BATCH_TILE = 16
SSM_BLOCK_GROUPS = 8
SSM_N_BLOCKS = SSM_GROUPS // SSM_BLOCK_GROUPS
SSM_BLOCK_STATE = SSM_BLOCK_GROUPS * SSM_STATE
SCAN_LANES = 512

VMEM_LIMIT_BYTES = 56 * 1024 * 1024
BF16_ROWS = 16


def _params(semantics, flags=None):
    return pltpu.CompilerParams(dimension_semantics=semantics, vmem_limit_bytes=VMEM_LIMIT_BYTES, flags=flags)


def _const_spec(shape):
    zeros = (0,) * len(shape)
    return pl.BlockSpec(shape, lambda *_: zeros)


def _gelu(x):
    return jax.nn.gelu(x, approximate=True)


_SEG_U = (0, D_MODEL)
_SEG_Q = (D_MODEL, D_MODEL + Q_WIDTH)
_SEG_KV0 = D_MODEL + Q_WIDTH
_SEG_GA = _SEG_KV0 + N_GROUPS * KV_WIDTH
_SEG_GB = _SEG_GA + D_MODEL


LANE_SLABS = D_MODEL // 128


def _in_proj_body(x_ref, g_ref, w_ref, u_ref, *refs, nb, ts, dilated):
    x = x_ref[...].reshape(nb * ts, D_MODEL)
    ms = jnp.mean(x * x, axis=-1, keepdims=True)
    xn = (x * lax.rsqrt(ms + EPS)) * g_ref[...]
    xb = xn.astype(BF16)
    scale = HEAD_DIM ** -0.5

    def seg(lhs, a, b):
        return jnp.dot(lhs, w_ref[:, a:b], preferred_element_type=F32)

    u = seg(xb, *_SEG_U)
    for k in range(nb):
        u_ref[:, k * D_MODEL:(k + 1) * D_MODEL] = u[k * ts:(k + 1) * ts]
    kv_cols = lambda gi: (_SEG_KV0 + gi * KV_WIDTH, _SEG_KV0 + (gi + 1) * KV_WIDTH)
    q_cols = lambda gi: (D_MODEL + gi * HEAD_LANES, D_MODEL + (gi + 1) * HEAD_LANES)

    if not dilated:
        q_ref, kv_refs, (ga_ref, gb_ref) = refs[0], refs[1:4], refs[4:6]
        q_ref[...] = (seg(xb, *_SEG_Q) * scale).reshape(nb, ts, Q_WIDTH)
        for gi in range(N_GROUPS):
            kv_refs[gi][...] = seg(xb, *kv_cols(gi)).reshape(nb, ts, KV_WIDTH)
    else:
        qd_refs, kvd_refs, kv_refs, (ga_ref, gb_ref), (xs, ys) = refs[0:3], refs[3:6], refs[6:9], refs[9:11], refs[11:13]
        for c in range(LANE_SLABS):
            xs[c] = xn[:, c * 128:(c + 1) * 128]
        for gi in range(N_GROUPS):
            d = DILATIONS[gi]
            n = ts // d
            if d == 1:
                lhs = xb
            else:
                lhs = jnp.concatenate(
                    [jnp.concatenate([xs[c, pl.ds(r, n, stride=d), :] for c in range(LANE_SLABS)], axis=1)
                     for r in range(d)], axis=0).astype(BF16)
            qd_refs[gi][0] = (seg(lhs, *q_cols(gi)) * scale).reshape(d, n, HEAD_LANES).astype(BF16)
            kv = seg(lhs, *kv_cols(gi))
            kvd_refs[gi][0] = kv.reshape(d, n, KV_WIDTH).astype(BF16)
            if d == 1:
                kv_refs[gi][0] = kv
            else:
                for r in range(d):
                    for c in range(LANE_SLABS):
                        ys[c, pl.ds(r, n, stride=d), :] = kv[r * n:(r + 1) * n, c * 128:(c + 1) * 128]
                kv_refs[gi][0] = jnp.concatenate([ys[c] for c in range(LANE_SLABS)], axis=1)
    ga_ref[...] = seg(xb, _SEG_GA, _SEG_GB).reshape(nb, ts, D_MODEL)
    gb_ref[...] = seg(xb, _SEG_GB, PROJ_WIDTH).reshape(nb, ts, D_MODEL)


def _in_proj(x, g_mix, w_perm, *, nb, ts, dilated):
    B, S, _ = x.shape
    assert not dilated or nb == 1
    row = lambda w: pl.BlockSpec((nb, ts, w), lambda b, s: (b, s, 0))
    f32_rows = lambda w: jax.ShapeDtypeStruct((B, S, w), F32)
    natural = [(row(KV_WIDTH), f32_rows(KV_WIDTH))] * N_GROUPS + [(row(D_MODEL), f32_rows(D_MODEL))] * 2
    if dilated:
        def regrouped(w):
            return [(pl.BlockSpec((1, d, ts // d, w), lambda b, s: (b, 0, s, 0)),
                     jax.ShapeDtypeStruct((B, d, S // d, w), BF16)) for d in DILATIONS]
        outs = regrouped(HEAD_LANES) + regrouped(KV_WIDTH) + natural
        scratch = [pltpu.VMEM((LANE_SLABS, ts, 128), F32), pltpu.VMEM((LANE_SLABS, ts, 128), F32)]
    else:
        outs = [(row(Q_WIDTH), f32_rows(Q_WIDTH))] + natural
        scratch = []
    outs = [(pl.BlockSpec((ts, nb * D_MODEL), lambda b, s: (s, b)), jax.ShapeDtypeStruct((S, B * D_MODEL), F32))] + outs
    return pl.pallas_call(
        functools.partial(_in_proj_body, nb=nb, ts=ts, dilated=dilated),
        grid=(B // nb, S // ts),
        in_specs=[row(D_MODEL), _const_spec((1, D_MODEL)),
                  pl.BlockSpec((D_MODEL, PROJ_WIDTH), lambda b, s: (0, 0), pipeline_mode=pl.Buffered(1))],
        out_specs=tuple(spec for spec, _ in outs),
        out_shape=tuple(shape for _, shape in outs),
        scratch_shapes=scratch,
        compiler_params=_params(("parallel", "parallel")),
        name="in_proj",
    )(x, g_mix.reshape(1, D_MODEL), w_perm)


def _ssm_body(u_ref, h0re_ref, h0im_ref, wx_ref, cy_ref, are_ref, aim_ref, dsk_ref,
              hs_ref, hre_ref, him_ref, xre, xim, st_re, st_im, *, ts):
    si = pl.program_id(1)
    rows = ts * SSM_BATCH_TILE

    @pl.when(si == 0)
    def _():
        st_re[...] = h0re_ref[...]
        st_im[...] = h0im_ref[...]

    u = u_ref[...].reshape(rows, D_MODEL)
    ub = u.astype(BF16)
    for j in range(SSM_N_BLOCKS):
        r = jnp.dot(ub[:, j * 128:(j + 1) * 128], wx_ref[j], preferred_element_type=F32)
        cols = slice(j * SSM_BLOCK_STATE, (j + 1) * SSM_BLOCK_STATE)
        xre[:, cols] = r[:, :SSM_BLOCK_STATE]
        xim[:, cols] = r[:, SSM_BLOCK_STATE:]

    for c in range(SSM_LANES // SCAN_LANES):
        cols = slice(c * SCAN_LANES, (c + 1) * SCAN_LANES)
        a_re = jnp.broadcast_to(are_ref[:, cols], (SSM_BATCH_TILE, SCAN_LANES))
        a_im = jnp.broadcast_to(aim_ref[:, cols], (SSM_BATCH_TILE, SCAN_LANES))

        def step(t, carry, cols=cols, a_re=a_re, a_im=a_im):
            h_re, h_im = carry
            r0 = pl.multiple_of(t * SSM_BATCH_TILE, SSM_BATCH_TILE)
            n_re = a_re * h_re - a_im * h_im + xre[pl.ds(r0, SSM_BATCH_TILE), cols]
            n_im = a_re * h_im + a_im * h_re + xim[pl.ds(r0, SSM_BATCH_TILE), cols]
            xre[pl.ds(r0, SSM_BATCH_TILE), cols] = n_re
            xim[pl.ds(r0, SSM_BATCH_TILE), cols] = n_im
            return n_re, n_im

        h_re, h_im = lax.fori_loop(0, ts, step, (st_re[:, cols], st_im[:, cols]), unroll=min(ts, 8))
        st_re[:, cols] = h_re
        st_im[:, cols] = h_im

    for j in range(SSM_N_BLOCKS):
        cols = slice(j * SSM_BLOCK_STATE, (j + 1) * SSM_BLOCK_STATE)
        h = jnp.concatenate([xre[:, cols], xim[:, cols]], axis=1).astype(BF16)
        y = jnp.dot(h, cy_ref[j], preferred_element_type=F32)
        ch = slice(j * 128, (j + 1) * 128)
        y = y + dsk_ref[:, ch] * u[:, ch]
        hs_ref[:, :, ch] = _gelu(y).reshape(ts, SSM_BATCH_TILE, 128)

    @pl.when(si == pl.num_programs(1) - 1)
    def _():
        hre_ref[...] = st_re[...]
        him_ref[...] = st_im[...]


def _ssm(u_t, h0_re, h0_im, wx, cy, a_re, a_im, dsk, *, ts):
    S = u_t.shape[0]
    B = u_t.shape[1] // D_MODEL
    nbt = B // SSM_BATCH_TILE
    u4 = u_t.reshape(S, nbt, SSM_BATCH_TILE, D_MODEL)
    tile = pl.BlockSpec((ts, None, SSM_BATCH_TILE, D_MODEL), lambda b, s: (s, b, 0, 0))
    state = pl.BlockSpec((SSM_BATCH_TILE, SSM_LANES), lambda b, s: (b, 0))
    rows = ts * SSM_BATCH_TILE
    hs, hre, him = pl.pallas_call(
        functools.partial(_ssm_body, ts=ts),
        grid=(nbt, S // ts),
        in_specs=[tile, state, state,
                  _const_spec((SSM_N_BLOCKS, 128, 2 * SSM_BLOCK_STATE)),
                  _const_spec((SSM_N_BLOCKS, 2 * SSM_BLOCK_STATE, 128)),
                  _const_spec((1, SSM_LANES)), _const_spec((1, SSM_LANES)), _const_spec((1, D_MODEL))],
        out_specs=(tile, state, state),
        out_shape=(jax.ShapeDtypeStruct(u4.shape, F32),
                   jax.ShapeDtypeStruct((B, SSM_LANES), F32), jax.ShapeDtypeStruct((B, SSM_LANES), F32)),
        scratch_shapes=[pltpu.VMEM((rows, SSM_LANES), F32), pltpu.VMEM((rows, SSM_LANES), F32),
                        pltpu.VMEM((SSM_BATCH_TILE, SSM_LANES), F32), pltpu.VMEM((SSM_BATCH_TILE, SSM_LANES), F32)],
        compiler_params=_params(("parallel", "arbitrary")),
        name="ssm",
    )(u4, h0_re, h0_im, wx, cy, a_re, a_im, dsk)
    return hs.reshape(S, B * D_MODEL), hre, him


def _ssm_weights(lam_re, lam_im, log_dt, b_re, b_im, c_re, c_im, d_skip):
    dt = jnp.exp(log_dt)[:, None]
    mag = jnp.exp(lam_re * dt)
    ang = lam_im * dt
    ab_re, ab_im = mag * jnp.cos(ang), mag * jnp.sin(ang)
    den = lam_re * lam_re + lam_im * lam_im
    f_re = ((ab_re - 1.0) * lam_re + ab_im * lam_im) / den
    f_im = (ab_im * lam_re - (ab_re - 1.0) * lam_im) / den
    bb_re = f_re[..., None] * b_re - f_im[..., None] * b_im
    bb_im = f_re[..., None] * b_im + f_im[..., None] * b_re
    eye = jnp.eye(SSM_BLOCK_GROUPS, dtype=F32)

    def in_block(bb):
        bb = bb.reshape(SSM_N_BLOCKS, SSM_BLOCK_GROUPS, SSM_STATE, SSM_CH)
        return jnp.einsum('jgnc,gh->jgchn', bb, eye).reshape(SSM_N_BLOCKS, 128, SSM_BLOCK_STATE)

    def out_block(cc):
        cc = cc.reshape(SSM_N_BLOCKS, SSM_BLOCK_GROUPS, SSM_CH, SSM_STATE)
        return jnp.einsum('jgcn,gh->jgnhc', cc, eye).reshape(SSM_N_BLOCKS, SSM_BLOCK_STATE, 128)

    wx = jnp.concatenate([in_block(bb_re), in_block(bb_im)], axis=2).astype(BF16)
    cy = jnp.concatenate([out_block(c_re), out_block(-c_im)], axis=1).astype(BF16)
    return (wx, cy, ab_re.reshape(1, SSM_LANES), ab_im.reshape(1, SSM_LANES), d_skip.reshape(1, D_MODEL))


def _head_masks():
    lane = lax.broadcasted_iota(jnp.int32, (1, 128), 1)
    return [(lane // HEAD_DIM) == hh for hh in range(2)]


def _attn_body(q_ref, kc_ref, kp_ref, o_ref, lse_ref):
    n = pl.program_id(2)
    q = q_ref[0, 0]
    kvc = kc_ref[0, 0]
    kvp = kp_ref[0, 0]
    kcat = jnp.concatenate([kvp[:, :HEAD_LANES], kvc[:, :HEAD_LANES]], axis=0)
    vcat = jnp.concatenate([kvp[:, HEAD_LANES:], kvc[:, HEAD_LANES:]], axis=0)
    qi = lax.broadcasted_iota(jnp.int32, (2 * N_STEPS, 2 * N_STEPS), 0) % N_STEPS
    ki = lax.broadcasted_iota(jnp.int32, (2 * N_STEPS, 2 * N_STEPS), 1)
    first_key = jnp.where(n > 0, 0, N_STEPS)
    mask = (ki >= jnp.maximum(qi, first_key)) & (ki <= qi + N_STEPS)
    hm = _head_masks()
    lane = lax.broadcasted_iota(jnp.int32, (1, 128), 1)
    lse_tile = jnp.zeros((N_STEPS, 128), F32)
    for hp in range(HEADS // 2):
        ch = slice(hp * 128, (hp + 1) * 128)
        qp, kp, vp = q[:, ch], kcat[:, ch], vcat[:, ch]
        zeros = jnp.zeros_like(qp)
        q2 = jnp.concatenate([jnp.where(hm[0], qp, zeros), jnp.where(hm[1], qp, zeros)], axis=0)
        s = lax.dot_general(q2, kp, _NT, preferred_element_type=F32)
        s = jnp.where(mask, s, NEG_INF)
        m = jnp.max(s, axis=-1, keepdims=True)
        p = jnp.exp(s - m)
        den = jnp.sum(p, axis=-1, keepdims=True)
        o2 = jnp.dot((p / den).astype(BF16), vp, preferred_element_type=F32)
        lse = m + jnp.log(den)
        for hh in range(2):
            lse_tile = jnp.where((lane // 16) == (2 * hp + hh), lse[hh * N_STEPS:(hh + 1) * N_STEPS], lse_tile)
        o_ref[0, 0, :, ch] = jnp.where(hm[0], o2[:N_STEPS], o2[N_STEPS:])
    lse_ref[0, 0] = lse_tile


def _attn_prompt(q, kv, gi):
    B, d, L, _ = q.shape
    nblk = L // N_STEPS
    blk = lambda w, prev: pl.BlockSpec((1, 1, N_STEPS, w),
                                       (lambda b, r, n: (b, r, jnp.maximum(n - 1, 0), 0)) if prev
                                       else (lambda b, r, n: (b, r, n, 0)))
    return pl.pallas_call(
        _attn_body,
        grid=(B, d, nblk),
        in_specs=[blk(HEAD_LANES, False), blk(KV_WIDTH, False), blk(KV_WIDTH, True)],
        out_specs=(blk(HEAD_LANES, False), blk(128, False)),
        out_shape=(jax.ShapeDtypeStruct((B, d, L, HEAD_LANES), F32),
                   jax.ShapeDtypeStruct((B, d, L, 128), F32)),
        compiler_params=_params(("parallel", "parallel", "arbitrary")),
        name=f"attn_prompt_g{gi}",
    )(q, kv, kv)


def _combine_body(o0_ref, o1_ref, o2_ref, l0_ref, l1_ref, l2_ref, ex_ref, attn_ref, oslab, lslab, *, tm):
    def natural(ref, gi, slab):
        d = DILATIONS[gi]
        if d == 1:
            return ref[0, 0]
        n = tm // d
        nslab = ref.shape[-1] // 128
        for r in range(d):
            for c in range(nslab):
                slab[c, pl.ds(r, n, stride=d), :] = ref[0, r, :, c * 128:(c + 1) * 128]
        return jnp.concatenate([slab[c] for c in range(nslab)], axis=1)

    o_refs, l_refs = (o0_ref, o1_ref, o2_ref), (l0_ref, l1_ref, l2_ref)
    ls = [natural(l_refs[gi], gi, lslab) for gi in range(N_GROUPS)]
    m = jnp.maximum(jnp.maximum(ls[0], ls[1]), ls[2])
    es = [jnp.exp(l - m) for l in ls]
    den = es[0] + es[1] + es[2]
    acc = None
    for gi in range(N_GROUPS):
        w = es[gi] / den
        w_hi = w.astype(BF16)
        w_lo = (w - w_hi.astype(F32)).astype(BF16)
        wx = (jnp.dot(w_hi, ex_ref[...], preferred_element_type=F32)
              + jnp.dot(w_lo, ex_ref[...], preferred_element_type=F32))
        term = wx * natural(o_refs[gi], gi, oslab)
        acc = term if acc is None else acc + term
    attn_ref[0] = acc.astype(BF16)


def _head_expand_matrix():
    src = jnp.arange(128)[:, None]
    dst = jnp.arange(HEAD_LANES)[None, :]
    return (src == (dst // HEAD_DIM) * 16).astype(BF16)


def _combine(os_, lses, *, tm):
    B = os_[0].shape[0]
    S = os_[0].shape[1] * os_[0].shape[2]
    blk = lambda d, w: pl.BlockSpec((1, d, tm // d, w), lambda b, s: (b, 0, s, 0))
    return pl.pallas_call(
        functools.partial(_combine_body, tm=tm),
        grid=(B, S // tm),
        in_specs=[blk(d, HEAD_LANES) for d in DILATIONS] + [blk(d, 128) for d in DILATIONS]
                 + [_const_spec((128, HEAD_LANES))],
        out_specs=pl.BlockSpec((1, tm, HEAD_LANES), lambda b, s: (b, s, 0)),
        out_shape=jax.ShapeDtypeStruct((B, S, HEAD_LANES), BF16),
        scratch_shapes=[pltpu.VMEM((HEAD_LANES // 128, tm, 128), F32), pltpu.VMEM((1, tm, 128), F32)],
        compiler_params=_params(("parallel", "parallel")),
        name="attn_combine",
    )(*os_, *lses, _head_expand_matrix())


SAMPLE_T = 8
_NT = (((1,), (1,)), ((), ()))


def _attn_sample_body(q_ref, n0_ref, n1_ref, n2_ref, c0_ref, c1_ref, c2_ref, attn_ref):
    q = q_ref[0]
    new_refs = (n0_ref, n1_ref, n2_ref)
    cache_refs = (c0_ref, c1_ref, c2_ref)
    hm = _head_masks()
    rows = 2 * SAMPLE_T

    def stride_mask(shape, d, lower):
        t = lax.broadcasted_iota(jnp.int32, shape, 0) % SAMPLE_T
        p = lax.broadcasted_iota(jnp.int32, shape, 1)
        back = (t - p) if lower else (p - t)
        return (back >= 0) & ((back & (d - 1)) == 0)

    cache_ok = [stride_mask((rows, WINDOWS[gi]), DILATIONS[gi], False) for gi in range(N_GROUPS)]
    new_ok = [stride_mask((rows, SAMPLE_T), DILATIONS[gi], True) for gi in range(N_GROUPS)]

    for hp in range(HEADS // 2):
        ch = slice(hp * 128, (hp + 1) * 128)
        scores = []
        for gi in range(N_GROUPS):
            qp = q[:, gi * HEAD_LANES + hp * 128:gi * HEAD_LANES + (hp + 1) * 128]
            lhs = jnp.concatenate([jnp.where(hm[0], qp, 0.0), jnp.where(hm[1], qp, 0.0)], axis=0).astype(BF16)
            kt = cache_refs[gi][0, 0, 2 * hp:2 * hp + 2].reshape(128, WINDOWS[gi]).astype(BF16)
            s = jnp.dot(lhs, kt, preferred_element_type=F32)
            kn = new_refs[gi][0, :, ch].astype(BF16)
            sn = lax.dot_general(lhs, kn, _NT, preferred_element_type=F32)
            scores.append((jnp.where(cache_ok[gi], s, NEG_INF), jnp.where(new_ok[gi], sn, NEG_INF)))
        m = functools.reduce(jnp.maximum, [jnp.max(x, axis=-1, keepdims=True) for pair in scores for x in pair])
        den = jnp.zeros((rows, 1), F32)
        o = jnp.zeros((rows, 128), F32)
        for gi in range(N_GROUPS):
            p = jnp.exp(scores[gi][0] - m)
            pn = jnp.exp(scores[gi][1] - m)
            den = den + jnp.sum(p, axis=-1, keepdims=True) + jnp.sum(pn, axis=-1, keepdims=True)
            vt = cache_refs[gi][0, 1, 2 * hp:2 * hp + 2].reshape(128, WINDOWS[gi]).astype(BF16)
            vn = new_refs[gi][0, :, HEAD_LANES + hp * 128:HEAD_LANES + (hp + 1) * 128].astype(BF16)
            o = o + lax.dot_general(p.astype(BF16), vt, _NT, preferred_element_type=F32)
            o = o + jnp.dot(pn.astype(BF16), vn, preferred_element_type=F32)
        o = o / den
        attn_ref[0, :, ch] = jnp.where(hm[0], o[:SAMPLE_T], o[SAMPLE_T:])


def _attn_sample(q, kv_new, caches_t):
    B = q.shape[0]
    specs = []
    for gi, c in enumerate(caches_t):
        assert c.shape[1:] == (2, HEADS, HEAD_DIM, WINDOWS[gi]), "decode caches must hold a full window"
        specs.append(pl.BlockSpec((1,) + c.shape[1:], lambda b: (b, 0, 0, 0, 0)))
    new_spec = pl.BlockSpec((1, SAMPLE_T, KV_WIDTH), lambda b: (b, 0, 0))
    return pl.pallas_call(
        _attn_sample_body,
        grid=(B,),
        in_specs=[pl.BlockSpec((1, SAMPLE_T, Q_WIDTH), lambda b: (b, 0, 0)), new_spec, new_spec, new_spec, *specs],
        out_specs=pl.BlockSpec((1, SAMPLE_T, HEAD_LANES), lambda b: (b, 0, 0)),
        out_shape=jax.ShapeDtypeStruct((B, SAMPLE_T, HEAD_LANES), F32),
        compiler_params=_params(("parallel",)),
        name="attn_sample",
    )(q, *kv_new, *caches_t)


def _merge_body(hs_ref, attn_ref, ga_ref, gb_ref, x_ref, wa_ref, wb_ref, wp_ref, wo_ref, gf_ref, wq_ref, sk_ref,
                x1_ref, xn_ref, st_ref, *, nb, ts):
    rows = nb * ts
    hs = jnp.concatenate([hs_ref[:, k * D_MODEL:(k + 1) * D_MODEL] for k in range(nb)], axis=0).astype(BF16)
    dot = lambda a, w_ref: jnp.dot(a, w_ref[...], preferred_element_type=F32)
    branch_a = dot(hs, wa_ref) * jax.nn.sigmoid(dot(hs, wb_ref))
    branch_b = dot(attn_ref[...].reshape(rows, HEAD_LANES).astype(BF16), wp_ref)
    ga = ga_ref[...].reshape(rows, D_MODEL)
    gb = gb_ref[...].reshape(rows, D_MODEL)
    mix = jax.nn.sigmoid(ga) * branch_a + jax.nn.sigmoid(gb) * branch_b
    x1 = x_ref[...].reshape(rows, D_MODEL) + dot(mix.astype(BF16), wo_ref)
    x1_ref[...] = x1
    ms = jnp.mean(x1 * x1, axis=-1, keepdims=True)
    xn = ((x1 * lax.rsqrt(ms + EPS)) * gf_ref[...]).astype(BF16)
    xn_ref[...] = xn
    qp = dot(xn, wq_ref).astype(BF16)
    for hp in range(2 * PEER_HEADS):
        st_ref[hp] = lax.dot_general(sk_ref[hp], qp[:, hp * PEER_HALF:(hp + 1) * PEER_HALF], _NT,
                                     preferred_element_type=F32)


def _merge(hs_t, attn, ga, gb, x, wa, wb, wp, wo, g_ffn, wq, sk, *, nb, ts):
    B, S, _ = x.shape
    rows = nb * ts
    T = B * S
    nsb = S // ts
    row = lambda w: pl.BlockSpec((nb, ts, w), lambda b, s: (b, s, 0))
    flat = lambda w: pl.BlockSpec((rows, w), lambda b, s: (b * nsb + s, 0))
    return pl.pallas_call(
        functools.partial(_merge_body, nb=nb, ts=ts),
        grid=(B // nb, nsb),
        in_specs=[pl.BlockSpec((ts, nb * D_MODEL), lambda b, s: (s, b)), row(HEAD_LANES), row(D_MODEL),
                  row(D_MODEL), row(D_MODEL),
                  _const_spec((D_MODEL, D_MODEL)), _const_spec((D_MODEL, D_MODEL)),
                  _const_spec((HEAD_LANES, D_MODEL)), _const_spec((D_MODEL, D_MODEL)),
                  _const_spec((1, D_MODEL)), _const_spec((D_MODEL, 2 * PEER_HEADS * PEER_HALF)),
                  _const_spec((2 * PEER_HEADS, N_KEYS, PEER_HALF))],
        out_specs=(flat(D_MODEL), flat(D_MODEL),
                   pl.BlockSpec((2 * PEER_HEADS, N_KEYS, rows), lambda b, s: (0, 0, b * nsb + s))),
        out_shape=(jax.ShapeDtypeStruct((T, D_MODEL), F32), jax.ShapeDtypeStruct((T, D_MODEL), BF16),
                   jax.ShapeDtypeStruct((2 * PEER_HEADS, N_KEYS, T), F32)),
        compiler_params=_params(("parallel", "parallel")),
        name="merge",
    )(hs_t, attn, ga, gb, x, wa, wb, wp, wo, g_ffn.reshape(1, D_MODEL), wq, sk)


ROUTE_LANES = 128
_PAIR_SLABS = ((0, 0, 0, 1, 8), (0, 0, 8, 1, 8), (1, 0, 0, 1, 8), (2, 0, 0, 1, 5), (3, 0, 0, 1, 4),
               (4, 0, 0, 1, 3), (5, 0, 0, 1, 2), (6, 0, 0, 1, 2), (7, 0, 0, 1, 2), (8, 1, 0, 0, 8))


def _top16(s, break_ties):
    idx = lax.broadcasted_iota(jnp.int32, s.shape, 0).astype(F32)
    cur = s
    rank = jnp.full(s.shape, float(PEER_TOPK), F32)
    vals = []
    for r in range(PEER_TOPK):
        m = jnp.max(cur, axis=0, keepdims=True)
        sel = cur == m
        if break_ties:
            sel = idx == jnp.min(jnp.where(sel, idx, float(N_KEYS)), axis=0, keepdims=True)
        rank = jnp.where(sel, float(r), rank)
        cur = jnp.where(sel, NEG_INF, cur)
        vals.append(m)
    n_ranked = jnp.sum(jnp.where(rank < float(PEER_TOPK), 1.0, 0.0), axis=0, keepdims=True)
    return rank, vals, n_ranked


def _route_head(s0, s1, break_ties):
    sub = lax.broadcasted_iota(jnp.int32, (8, ROUTE_LANES), 0)
    rank0, v0, n0 = _top16(s0, break_ties)
    rank1, v1, n1 = _top16(s1, break_ties)
    v0b = jnp.concatenate(v0[8:], axis=0)
    v1a, v1b = jnp.concatenate(v1[:8], axis=0), jnp.concatenate(v1[8:], axis=0)

    cands, cidx = [], []
    for (r0, r0s, r1, r1s, nrow) in _PAIR_SLABS:
        a = (v0b if r0 == 8 else v0[r0])
        b = (v1[0] if r1s == 0 else (v1a if r1 == 0 else v1b))
        c = a + b
        if nrow < 8:
            c = jnp.where(sub < nrow, c, NEG_INF)
        cands.append(c)
        cidx.append(((r0 + r0s * sub) * PEER_TOPK + (r1 + r1s * sub)).astype(F32))

    cur = list(cands)
    picked = [jnp.zeros((8, ROUTE_LANES), F32) for _ in cands]
    big = float(PEER_TOPK * PEER_TOPK)
    for _ in range(PEER_TOPK):
        m = jnp.max(functools.reduce(jnp.maximum, cur), axis=0, keepdims=True)
        sels = [c == m for c in cur]
        if break_ties:
            first = functools.reduce(jnp.minimum, [jnp.where(s, ci, big) for s, ci in zip(sels, cidx)])
            first = jnp.min(first, axis=0, keepdims=True)
            sels = [ci == first for ci in cidx]
        for k, sel in enumerate(sels):
            picked[k] = jnp.where(sel, 1.0, picked[k])
            cur[k] = jnp.where(sel, NEG_INF, cur[k])

    total = functools.reduce(lambda x, y: x + y, picked)
    n_pairs = jnp.sum(total, axis=0, keepdims=True)
    top = v0[0] + v1[0]
    z = functools.reduce(lambda x, y: x + y,
                         [jnp.sum(pk * jnp.exp(c - top), axis=0, keepdims=True) for pk, c in zip(picked, cands)])
    counts = [jnp.sum(picked[0] + picked[1], axis=0, keepdims=True)]
    counts += [jnp.sum(picked[k], axis=0, keepdims=True) for k in range(2, 9)]
    counts += [picked[9][i:i + 1, :] for i in range(8)]
    l0 = jnp.zeros((N_KEYS, ROUTE_LANES), F32)
    for r in range(PEER_TOPK):
        l0 = jnp.where(rank0 == float(r), counts[r], l0)

    k16 = float(PEER_TOPK)
    n_bad = jnp.abs(n0 - k16) + jnp.abs(n1 - k16) + jnp.abs(n_pairs - k16)
    return jnp.exp(s0 - v0[0]) / z, l0, jnp.exp(s1 - v1[0]), rank1, n_bad


ROUTE_HEADS_PER_ITER = 4


def _bf16_pair_words(x):
    w = lax.bitcast_convert_type(x.astype(BF16).astype(F32), jnp.uint32)
    return w | (w >> 16)


def _route_body(st_ref, c0_ref, l0_ref, b1_ref, r1_ref):
    def heads(it, carry):
        def run(break_ties):
            n_bad = None
            for k in range(ROUTE_HEADS_PER_ITER):
                h = it * ROUTE_HEADS_PER_ITER + k
                c0, l0, b1, r1, bad = _route_head(st_ref[2 * h], st_ref[2 * h + 1], break_ties)
                c0_ref[h] = _bf16_pair_words(c0)
                l0_ref[h] = _bf16_pair_words(l0)
                b1_ref[h] = b1.astype(BF16)
                r1_ref[h] = r1.astype(BF16)
                n_bad = bad if n_bad is None else n_bad + bad
            return jnp.max(n_bad)

        n_bad = run(False)

        @pl.when(n_bad > 0.0)
        def _():
            run(True)

        return carry

    lax.fori_loop(0, PEER_HEADS // ROUTE_HEADS_PER_ITER, heads, 0)


def _route(st):
    T = st.shape[2]
    out = lambda dt: jax.ShapeDtypeStruct((PEER_HEADS, N_KEYS, T), dt)
    spec = pl.BlockSpec((PEER_HEADS, N_KEYS, ROUTE_LANES), lambda i: (0, 0, i))
    return pl.pallas_call(
        _route_body,
        grid=(T // ROUTE_LANES,),
        in_specs=[pl.BlockSpec((2 * PEER_HEADS, N_KEYS, ROUTE_LANES), lambda i: (0, 0, i))],
        out_specs=(spec, spec, spec, spec),
        out_shape=(out(jnp.uint32), out(jnp.uint32), out(BF16), out(BF16)),
        compiler_params=_params(("parallel",)),
        name="peer_route",
    )(st)


PEER_EXPERT_BLOCK = 1024
PEER_I_PER_BLOCK = PEER_EXPERT_BLOCK // N_KEYS
GELU_C = math.sqrt(2.0 / math.pi)


def _peer_body(xn_ref, u_ref, vt_ref, c0_ref, l0_ref, b1_ref, r1_ref, x1_ref, gfin_ref, y_ref, acc, pg):
    e = pl.program_id(1)

    @pl.when(e == 0)
    def _():
        acc[...] = jnp.zeros_like(acc)

    tt = xn_ref.shape[0]
    tile = (N_KEYS // BF16_ROWS, BF16_ROWS, tt)
    zero = jnp.zeros(tile, BF16)

    def row16(ref, h, ib):
        return pltpu.bitcast(jnp.broadcast_to(ref[h, ib:ib + 1, :], (8, tt)), BF16)

    def gated(ib, a):
        g = None
        for h in range(PEER_HEADS):
            term = jnp.where(r1_ref[h].reshape(tile) < row16(l0_ref, h, ib)[None], b1_ref[h].reshape(tile), zero)
            term = term * row16(c0_ref, h, ib)[None]
            g = term if g is None else g + term
        x = a.astype(BF16).reshape(tile)
        z = (x * x * (GELU_C * 0.044715) + GELU_C) * x
        hx = 0.5 * x
        return (g * (hx + hx * jnp.tanh(z))).reshape(N_KEYS, tt)

    act = lax.dot_general(u_ref[...], xn_ref[...], _NT, preferred_element_type=F32)
    for ib in range(PEER_I_PER_BLOCK):
        rows = slice(ib * N_KEYS, (ib + 1) * N_KEYS)
        pg[rows, :] = gated(ib, act[rows, :])
    acc[...] += jnp.dot(vt_ref[...], pg[...], preferred_element_type=F32)

    @pl.when(e == pl.num_programs(1) - 1)
    def _():
        x2 = x1_ref[...] + acc[...].T
        ms = jnp.mean(x2 * x2, axis=-1, keepdims=True)
        y_ref[...] = (x2 * lax.rsqrt(ms + EPS)) * gfin_ref[...]


def _peer(xn, u_bf, vt_bf, c0, l0, b1, r1, x1, g_final, *, tt):
    T = xn.shape[0]
    tok = pl.BlockSpec((tt, D_MODEL), lambda t, e: (t, 0))
    rt = pl.BlockSpec((PEER_HEADS, N_KEYS, tt), lambda t, e: (0, 0, t))
    rows = pl.BlockSpec((PEER_HEADS, PEER_I_PER_BLOCK, tt), lambda t, e: (0, e, t))
    return pl.pallas_call(
        _peer_body,
        grid=(T // tt, N_EXPERTS // PEER_EXPERT_BLOCK),
        in_specs=[tok, pl.BlockSpec((PEER_EXPERT_BLOCK, D_MODEL), lambda t, e: (e, 0)),
                  pl.BlockSpec((D_MODEL, PEER_EXPERT_BLOCK), lambda t, e: (0, e)),
                  rows, rows, rt, rt, tok, _const_spec((1, D_MODEL))],
        out_specs=tok,
        out_shape=jax.ShapeDtypeStruct((T, D_MODEL), F32),
        scratch_shapes=[pltpu.VMEM((D_MODEL, tt), F32), pltpu.VMEM((PEER_EXPERT_BLOCK, tt), BF16)],
        compiler_params=_params(("parallel", "arbitrary")),
        name="peer_dense",
    )(xn, u_bf, vt_bf, c0, l0, b1, r1, x1, g_final.reshape(1, D_MODEL))


def _pack_w_in(w_in):
    o_k = D_MODEL + Q_WIDTH
    o_v = o_k + Q_WIDTH
    o_g = o_v + Q_WIDTH
    parts = [w_in[:, :o_k]]
    for gi in range(N_GROUPS):
        parts.append(w_in[:, o_k + gi * HEAD_LANES:o_k + (gi + 1) * HEAD_LANES])
        parts.append(w_in[:, o_v + gi * HEAD_LANES:o_v + (gi + 1) * HEAD_LANES])
    parts.append(w_in[:, o_g:])
    return jnp.concatenate(parts, axis=1).astype(BF16)


def _layer(x, h0_re, h0_im, caches, wts, *, nb, ts, ssm_ts, peer_tt):
    B, S, _ = x.shape
    proj = _in_proj(x, wts["g_mix"], wts["w_in"], nb=nb, ts=ts, dilated=caches is None)
    u_t, kvs, (ga, gb) = proj[0], proj[-5:-2], proj[-2:]
    hs_t, hre, him = _ssm(u_t, h0_re, h0_im, *wts["ssm"], ts=ssm_ts)
    if caches is None:
        outs = [_attn_prompt(proj[1 + gi], proj[1 + N_GROUPS + gi], gi) for gi in range(N_GROUPS)]
        attn = _combine([o for o, _ in outs], [l for _, l in outs], tm=ts)
    else:
        attn = _attn_sample(proj[1], kvs, caches)
    x1, xn, st = _merge(hs_t, attn, ga, gb, x, wts["w_glu_a"], wts["w_glu_b"], wts["w_attn_proj"],
                        wts["w_out"], wts["g_ffn"], wts["w_qp"], wts["sub_keys"], nb=nb, ts=ts)
    c0, l0, b1, r1 = _route(st)
    y = _peer(xn, wts["u_tab"], wts["v_tab_t"], c0, l0, b1, r1, x1, wts["g_final"], tt=peer_tt)
    return y.reshape(B, S, D_MODEL), hre, him, kvs


def kernel(x_prompt, x_sample, state_ssm_re, state_ssm_im, cache_kv_w128, cache_kv_w512, cache_kv_w2048, g_mix, w_in, lam_re, lam_im, log_dt, b_re, b_im, c_re, c_im, d_skip, w_glu_a, w_glu_b, w_attn_proj, w_out, g_ffn, w_qp, sub_keys, u_tab, v_tab, g_final):
    assert w_in.shape[0] == 1, "single-layer model"
    wts = {
        "g_mix": g_mix[0],
        "w_in": _pack_w_in(w_in[0]),
        "ssm": _ssm_weights(lam_re[0], lam_im[0], log_dt[0], b_re[0], b_im[0], c_re[0], c_im[0], d_skip[0]),
        "w_glu_a": w_glu_a[0].astype(BF16),
        "w_glu_b": w_glu_b[0].astype(BF16),
        "w_attn_proj": w_attn_proj[0].astype(BF16),
        "w_out": w_out[0].astype(BF16),
        "g_ffn": g_ffn[0],
        "w_qp": w_qp[0].astype(BF16),
        "sub_keys": sub_keys[0].reshape(2 * PEER_HEADS, N_KEYS, PEER_HALF).astype(BF16),
        "u_tab": u_tab[0].astype(BF16),
        "v_tab_t": v_tab[0].astype(BF16).T,
        "g_final": g_final,
    }
    Bp, Sp, _ = x_prompt.shape
    Bs, Ss, _ = x_sample.shape
    assert Ss == SAMPLE_T
    zeros = jnp.zeros((Bp, SSM_LANES), F32)
    yp, hre_p, him_p, kv_p = _layer(x_prompt, zeros, zeros, None, wts, nb=1, ts=256, ssm_ts=32, peer_tt=1024)
    caches = tuple(c[0].transpose(0, 2, 3, 4, 1) for c in (cache_kv_w128, cache_kv_w512, cache_kv_w2048))
    ys, hre_s, him_s, kv_s = _layer(x_sample, state_ssm_re[0].reshape(Bs, SSM_LANES),
                                    state_ssm_im[0].reshape(Bs, SSM_LANES), caches, wts,
                                    nb=16, ts=SAMPLE_T, ssm_ts=SAMPLE_T, peer_tt=1024)

    state = lambda h: h.reshape(1, -1, SSM_GROUPS, SSM_STATE)
    kv_rows = lambda kv, w: kv[:, kv.shape[1] - min(w, kv.shape[1]):].reshape(1, kv.shape[0], -1, 2, HEADS, HEAD_DIM)
    return (yp, ys, state(hre_p), state(him_p),
            kv_rows(kv_p[0], WINDOWS[0]), kv_rows(kv_p[1], WINDOWS[1]), kv_rows(kv_p[2], WINDOWS[2]),
            state(hre_s), state(him_s),
            kv_rows(kv_s[0], SAMPLE_T), kv_rows(kv_s[1], SAMPLE_T), kv_rows(kv_s[2], SAMPLE_T))
```

```python
import functools
import math

import jax
import jax.numpy as jnp
from jax import lax
from jax.experimental import pallas as pl
from jax.experimental.pallas import tpu as pltpu

F32 = jnp.float32
BF16 = jnp.bfloat16

D_MODEL = 1024
SSM_CH = 16
SSM_GROUPS = D_MODEL // SSM_CH
SSM_STATE = 64
SSM_LANES = SSM_GROUPS * SSM_STATE
HEAD_DIM = 64
HEADS = 8
HEAD_LANES = HEADS * HEAD_DIM
N_GROUPS = 3
WINDOWS = (128, 512, 2048)
DILATIONS = (1, 4, 16)
N_STEPS = 128
Q_WIDTH = N_GROUPS * HEAD_LANES
KV_WIDTH = 2 * HEAD_LANES
PROJ_WIDTH = D_MODEL + 3 * Q_WIDTH + 2 * D_MODEL
PEER_HEADS = 8
PEER_TOPK = 16
N_KEYS = 128
N_EXPERTS = N_KEYS * N_KEYS
PEER_HALF = 128
EPS = 1e-6
NEG_INF = float("-inf")

SSM_BATCH_TILE = 16
SSM_BLOCK_GROUPS = 8
SSM_N_BLOCKS = SSM_GROUPS // SSM_BLOCK_GROUPS
SSM_BLOCK_STATE = SSM_BLOCK_GROUPS * SSM_STATE
SCAN_LANES = 512

VMEM_LIMIT_BYTES = 56 * 1024 * 1024
BF16_ROWS = 16


def _params(semantics, flags=None):
    return pltpu.CompilerParams(dimension_semantics=semantics, vmem_limit_bytes=VMEM_LIMIT_BYTES, flags=flags)


def _const_spec(shape):
    zeros = (0,) * len(shape)
    return pl.BlockSpec(shape, lambda *_: zeros)


def _gelu(x):
    return jax.nn.gelu(x, approximate=True)


_SEG_U = (0, D_MODEL)
_SEG_Q = (D_MODEL, D_MODEL + Q_WIDTH)
_SEG_KV0 = D_MODEL + Q_WIDTH
_SEG_GA = _SEG_KV0 + N_GROUPS * KV_WIDTH
_SEG_GB = _SEG_GA + D_MODEL


LANE_SLABS = D_MODEL // 128


def _in_proj_body(x_ref, g_ref, w_ref, u_ref, *refs, nb, ts, dilated):
    x = x_ref[...].reshape(nb * ts, D_MODEL)
    ms = jnp.mean(x * x, axis=-1, keepdims=True)
    xn = (x * lax.rsqrt(ms + EPS)) * g_ref[...]
    xb = xn.astype(BF16)
    scale = HEAD_DIM ** -0.5

    def seg(lhs, a, b):
        return jnp.dot(lhs, w_ref[:, a:b], preferred_element_type=F32)

    u = seg(xb, *_SEG_U)
    for k in range(nb):
        u_ref[:, k * D_MODEL:(k + 1) * D_MODEL] = u[k * ts:(k + 1) * ts]
    kv_cols = lambda gi: (_SEG_KV0 + gi * KV_WIDTH, _SEG_KV0 + (gi + 1) * KV_WIDTH)
    q_cols = lambda gi: (D_MODEL + gi * HEAD_LANES, D_MODEL + (gi + 1) * HEAD_LANES)

    if not dilated:
        q_ref, kv_refs, (ga_ref, gb_ref) = refs[0], refs[1:4], refs[4:6]
        q_ref[...] = (seg(xb, *_SEG_Q) * scale).reshape(nb, ts, Q_WIDTH)
        for gi in range(N_GROUPS):
            kv_refs[gi][...] = seg(xb, *kv_cols(gi)).reshape(nb, ts, KV_WIDTH)
    else:
        qd_refs, kvd_refs, kv_refs, (ga_ref, gb_ref), (xs, ys) = refs[0:3], refs[3:6], refs[6:9], refs[9:11], refs[11:13]
        for c in range(LANE_SLABS):
            xs[c] = xn[:, c * 128:(c + 1) * 128]
        for gi in range(N_GROUPS):
            d = DILATIONS[gi]
            n = ts // d
            if d == 1:
                lhs = xb
            else:
                lhs = jnp.concatenate(
                    [jnp.concatenate([xs[c, pl.ds(r, n, stride=d), :] for c in range(LANE_SLABS)], axis=1)
                     for r in range(d)], axis=0).astype(BF16)
            qd_refs[gi][0] = (seg(lhs, *q_cols(gi)) * scale).reshape(d, n, HEAD_LANES).astype(BF16)
            kv = seg(lhs, *kv_cols(gi))
            kvd_refs[gi][0] = kv.reshape(d, n, KV_WIDTH).astype(BF16)
            if d == 1:
                kv_refs[gi][0] = kv
            else:
                for r in range(d):
                    for c in range(LANE_SLABS):
                        ys[c, pl.ds(r, n, stride=d), :] = kv[r * n:(r + 1) * n, c * 128:(c + 1) * 128]
                kv_refs[gi][0] = jnp.concatenate([ys[c] for c in range(LANE_SLABS)], axis=1)
    ga_ref[...] = seg(xb, _SEG_GA, _SEG_GB).reshape(nb, ts, D_MODEL)
    gb_ref[...] = seg(xb, _SEG_GB, PROJ_WIDTH).reshape(nb, ts, D_MODEL)


def _in_proj(x, g_mix, w_perm, *, nb, ts, dilated):
    B, S, _ = x.shape
    assert not dilated or nb == 1
    row = lambda w: pl.BlockSpec((nb, ts, w), lambda b, s: (b, s, 0))
    f32_rows = lambda w: jax.ShapeDtypeStruct((B, S, w), F32)
    natural = [(row(KV_WIDTH), f32_rows(KV_WIDTH))] * N_GROUPS + [(row(D_MODEL), f32_rows(D_MODEL))] * 2
    if dilated:
        def regrouped(w):
            return [(pl.BlockSpec((1, d, ts // d, w), lambda b, s: (b, 0, s, 0)),
                     jax.ShapeDtypeStruct((B, d, S // d, w), BF16)) for d in DILATIONS]
        outs = regrouped(HEAD_LANES) + regrouped(KV_WIDTH) + natural
        scratch = [pltpu.VMEM((LANE_SLABS, ts, 128), F32), pltpu.VMEM((LANE_SLABS, ts, 128), F32)]
    else:
        outs = [(row(Q_WIDTH), f32_rows(Q_WIDTH))] + natural
        scratch = []
    outs = [(pl.BlockSpec((ts, nb * D_MODEL), lambda b, s: (s, b)), jax.ShapeDtypeStruct((S, B * D_MODEL), F32))] + outs
    return pl.pallas_call(
        functools.partial(_in_proj_body, nb=nb, ts=ts, dilated=dilated),
        grid=(B // nb, S // ts),
        in_specs=[row(D_MODEL), _const_spec((1, D_MODEL)),
                  pl.BlockSpec((D_MODEL, PROJ_WIDTH), lambda b, s: (0, 0), pipeline_mode=pl.Buffered(1))],
        out_specs=tuple(spec for spec, _ in outs),
        out_shape=tuple(shape for _, shape in outs),
        scratch_shapes=scratch,
        compiler_params=_params(("parallel", "parallel")),
        name="in_proj",
    )(x, g_mix.reshape(1, D_MODEL), w_perm)


def _ssm_body(u_ref, h0re_ref, h0im_ref, wx_ref, cy_ref, are_ref, aim_ref, dsk_ref,
              hs_ref, hre_ref, him_ref, xre, xim, st_re, st_im, *, ts):
    si = pl.program_id(1)
    rows = ts * SSM_BATCH_TILE

    @pl.when(si == 0)
    def _():
        st_re[...] = h0re_ref[...]
        st_im[...] = h0im_ref[...]

    u = u_ref[...].reshape(rows, D_MODEL)
    ub = u.astype(BF16)
    for j in range(SSM_N_BLOCKS):
        r = jnp.dot(ub[:, j * 128:(j + 1) * 128], wx_ref[j], preferred_element_type=F32)
        cols = slice(j * SSM_BLOCK_STATE, (j + 1) * SSM_BLOCK_STATE)
        xre[:, cols] = r[:, :SSM_BLOCK_STATE]
        xim[:, cols] = r[:, SSM_BLOCK_STATE:]

    for c in range(SSM_LANES // SCAN_LANES):
        cols = slice(c * SCAN_LANES, (c + 1) * SCAN_LANES)
        a_re = jnp.broadcast_to(are_ref[:, cols], (SSM_BATCH_TILE, SCAN_LANES))
        a_im = jnp.broadcast_to(aim_ref[:, cols], (SSM_BATCH_TILE, SCAN_LANES))

        def step(t, carry, cols=cols, a_re=a_re, a_im=a_im):
            h_re, h_im = carry
            r0 = pl.multiple_of(t * SSM_BATCH_TILE, SSM_BATCH_TILE)
            n_re = a_re * h_re - a_im * h_im + xre[pl.ds(r0, SSM_BATCH_TILE), cols]
            n_im = a_re * h_im + a_im * h_re + xim[pl.ds(r0, SSM_BATCH_TILE), cols]
            xre[pl.ds(r0, SSM_BATCH_TILE), cols] = n_re
            xim[pl.ds(r0, SSM_BATCH_TILE), cols] = n_im
            return n_re, n_im

        h_re, h_im = lax.fori_loop(0, ts, step, (st_re[:, cols], st_im[:, cols]), unroll=min(ts, 8))
        st_re[:, cols] = h_re
        st_im[:, cols] = h_im

    for j in range(SSM_N_BLOCKS):
        cols = slice(j * SSM_BLOCK_STATE, (j + 1) * SSM_BLOCK_STATE)
        h = jnp.concatenate([xre[:, cols], xim[:, cols]], axis=1).astype(BF16)
        y = jnp.dot(h, cy_ref[j], preferred_element_type=F32)
        ch = slice(j * 128, (j + 1) * 128)
        y = y + dsk_ref[:, ch] * u[:, ch]
        hs_ref[:, :, ch] = _gelu(y).reshape(ts, SSM_BATCH_TILE, 128)

    @pl.when(si == pl.num_programs(1) - 1)
    def _():
        hre_ref[...] = st_re[...]
        him_ref[...] = st_im[...]


def _ssm(u_t, h0_re, h0_im, wx, cy, a_re, a_im, dsk, *, ts):
    S = u_t.shape[0]
    B = u_t.shape[1] // D_MODEL
    nbt = B // SSM_BATCH_TILE
    u4 = u_t.reshape(S, nbt, SSM_BATCH_TILE, D_MODEL)
    tile = pl.BlockSpec((ts, None, SSM_BATCH_TILE, D_MODEL), lambda b, s: (s, b, 0, 0))
    state = pl.BlockSpec((SSM_BATCH_TILE, SSM_LANES), lambda b, s: (b, 0))
    rows = ts * SSM_BATCH_TILE
    hs, hre, him = pl.pallas_call(
        functools.partial(_ssm_body, ts=ts),
        grid=(nbt, S // ts),
        in_specs=[tile, state, state,
                  _const_spec((SSM_N_BLOCKS, 128, 2 * SSM_BLOCK_STATE)),
                  _const_spec((SSM_N_BLOCKS, 2 * SSM_BLOCK_STATE, 128)),
                  _const_spec((1, SSM_LANES)), _const_spec((1, SSM_LANES)), _const_spec((1, D_MODEL))],
        out_specs=(tile, state, state),
        out_shape=(jax.ShapeDtypeStruct(u4.shape, F32),
                   jax.ShapeDtypeStruct((B, SSM_LANES), F32), jax.ShapeDtypeStruct((B, SSM_LANES), F32)),
        scratch_shapes=[pltpu.VMEM((rows, SSM_LANES), F32), pltpu.VMEM((rows, SSM_LANES), F32),
                        pltpu.VMEM((SSM_BATCH_TILE, SSM_LANES), F32), pltpu.VMEM((SSM_BATCH_TILE, SSM_LANES), F32)],
        compiler_params=_params(("parallel", "arbitrary")),
        name="ssm",
    )(u4, h0_re, h0_im, wx, cy, a_re, a_im, dsk)
    return hs.reshape(S, B * D_MODEL), hre, him


def _ssm_weights(lam_re, lam_im, log_dt, b_re, b_im, c_re, c_im, d_skip):
    dt = jnp.exp(log_dt)[:, None]
    mag = jnp.exp(lam_re * dt)
    ang = lam_im * dt
    ab_re, ab_im = mag * jnp.cos(ang), mag * jnp.sin(ang)
    den = lam_re * lam_re + lam_im * lam_im
    f_re = ((ab_re - 1.0) * lam_re + ab_im * lam_im) / den
    f_im = (ab_im * lam_re - (ab_re - 1.0) * lam_im) / den
    bb_re = f_re[..., None] * b_re - f_im[..., None] * b_im
    bb_im = f_re[..., None] * b_im + f_im[..., None] * b_re
    eye = jnp.eye(SSM_BLOCK_GROUPS, dtype=F32)

    def in_block(bb):
        bb = bb.reshape(SSM_N_BLOCKS, SSM_BLOCK_GROUPS, SSM_STATE, SSM_CH)
        return jnp.einsum('jgnc,gh->jgchn', bb, eye).reshape(SSM_N_BLOCKS, 128, SSM_BLOCK_STATE)

    def out_block(cc):
        cc = cc.reshape(SSM_N_BLOCKS, SSM_BLOCK_GROUPS, SSM_CH, SSM_STATE)
        return jnp.einsum('jgcn,gh->jgnhc', cc, eye).reshape(SSM_N_BLOCKS, SSM_BLOCK_STATE, 128)

    wx = jnp.concatenate([in_block(bb_re), in_block(bb_im)], axis=2).astype(BF16)
    cy = jnp.concatenate([out_block(c_re), out_block(-c_im)], axis=1).astype(BF16)
    return (wx, cy, ab_re.reshape(1, SSM_LANES), ab_im.reshape(1, SSM_LANES), d_skip.reshape(1, D_MODEL))


def _head_masks():
    lane = lax.broadcasted_iota(jnp.int32, (1, 128), 1)
    return [(lane // HEAD_DIM) == hh for hh in range(2)]


ATTN_UNROLL = 4


def _attn_body(q_ref, kv_ref, o_ref, lse_ref, *, nblk):
    d = q_ref.shape[1]
    window = min(2, nblk) * N_STEPS
    qi = lax.broadcasted_iota(jnp.int32, (2 * N_STEPS, window), 0) % N_STEPS
    ki = lax.broadcasted_iota(jnp.int32, (2 * N_STEPS, window), 1)
    hm = _head_masks()
    lane = lax.broadcasted_iota(jnp.int32, (1, 128), 1)

    def tile(t, carry):
        r = t // nblk
        n = t % nblk
        q0 = pl.multiple_of(n * N_STEPS, N_STEPS)
        k0 = pl.multiple_of(jnp.maximum(n - 1, 0) * N_STEPS, N_STEPS)
        q = q_ref[0, r, pl.ds(q0, N_STEPS), :]
        kv = kv_ref[0, r, pl.ds(k0, window), :]
        back = (q0 - k0) + qi - ki
        mask = (back >= 0) & (back <= N_STEPS)
        lse_tile = jnp.zeros((N_STEPS, 128), F32)
        for hp in range(HEADS // 2):
            ch = slice(hp * 128, (hp + 1) * 128)
            qp, kp, vp = q[:, ch], kv[:, ch], kv[:, HEAD_LANES + hp * 128:HEAD_LANES + (hp + 1) * 128]
            zeros = jnp.zeros_like(qp)
            q2 = jnp.concatenate([jnp.where(hm[0], qp, zeros), jnp.where(hm[1], qp, zeros)], axis=0)
            s = lax.dot_general(q2, kp, _NT, preferred_element_type=F32)
            s = jnp.where(mask, s, NEG_INF)
            m = jnp.max(s, axis=-1, keepdims=True)
            p = jnp.exp(s - m)
            den = jnp.sum(p, axis=-1, keepdims=True)
            o2 = jnp.dot((p / den).astype(BF16), vp, preferred_element_type=F32)
            lse = m + jnp.log(den)
            for hh in range(2):
                lse_tile = jnp.where((lane // 16) == (2 * hp + hh), lse[hh * N_STEPS:(hh + 1) * N_STEPS], lse_tile)
            o_ref[0, r, pl.ds(q0, N_STEPS), ch] = jnp.where(hm[0], o2[:N_STEPS], o2[N_STEPS:])
        lse_ref[0, r, pl.ds(q0, N_STEPS), :] = lse_tile
        return carry

    lax.fori_loop(0, d * nblk, tile, 0, unroll=ATTN_UNROLL)


def _attn_prompt(q, kv, gi):
    B, d, L, _ = q.shape
    blk = lambda w: pl.BlockSpec((1, d, L, w), lambda b: (b, 0, 0, 0))
    return pl.pallas_call(
        functools.partial(_attn_body, nblk=L // N_STEPS),
        grid=(B,),
        in_specs=[blk(HEAD_LANES), blk(KV_WIDTH)],
        out_specs=(blk(HEAD_LANES), blk(128)),
        out_shape=(jax.ShapeDtypeStruct((B, d, L, HEAD_LANES), F32),
                   jax.ShapeDtypeStruct((B, d, L, 128), F32)),
        compiler_params=_params(("parallel",)),
        name=f"attn_prompt_g{gi}",
    )(q, kv)


def _combine_body(o0_ref, o1_ref, o2_ref, l0_ref, l1_ref, l2_ref, ex_ref, attn_ref, oslab, lslab, *, tm):
    def natural(ref, gi, slab):
        d = DILATIONS[gi]
        if d == 1:
            return ref[0, 0]
        n = tm // d
        nslab = ref.shape[-1] // 128
        for r in range(d):
            for c in range(nslab):
                slab[c, pl.ds(r, n, stride=d), :] = ref[0, r, :, c * 128:(c + 1) * 128]
        return jnp.concatenate([slab[c] for c in range(nslab)], axis=1)

    o_refs, l_refs = (o0_ref, o1_ref, o2_ref), (l0_ref, l1_ref, l2_ref)
    ls = [natural(l_refs[gi], gi, lslab) for gi in range(N_GROUPS)]
    m = jnp.maximum(jnp.maximum(ls[0], ls[1]), ls[2])
    es = [jnp.exp(l - m) for l in ls]
    den = es[0] + es[1] + es[2]
    acc = None
    for gi in range(N_GROUPS):
        w = es[gi] / den
        w_hi = w.astype(BF16)
        w_lo = (w - w_hi.astype(F32)).astype(BF16)
        wx = (jnp.dot(w_hi, ex_ref[...], preferred_element_type=F32)
              + jnp.dot(w_lo, ex_ref[...], preferred_element_type=F32))
        term = wx * natural(o_refs[gi], gi, oslab)
        acc = term if acc is None else acc + term
    attn_ref[0] = acc.astype(BF16)


def _head_expand_matrix():
    src = jnp.arange(128)[:, None]
    dst = jnp.arange(HEAD_LANES)[None, :]
    return (src == (dst // HEAD_DIM) * 16).astype(BF16)


def _combine(os_, lses, *, tm):
    B = os_[0].shape[0]
    S = os_[0].shape[1] * os_[0].shape[2]
    blk = lambda d, w: pl.BlockSpec((1, d, tm // d, w), lambda b, s: (b, 0, s, 0))
    return pl.pallas_call(
        functools.partial(_combine_body, tm=tm),
        grid=(B, S // tm),
        in_specs=[blk(d, HEAD_LANES) for d in DILATIONS] + [blk(d, 128) for d in DILATIONS]
                 + [_const_spec((128, HEAD_LANES))],
        out_specs=pl.BlockSpec((1, tm, HEAD_LANES), lambda b, s: (b, s, 0)),
        out_shape=jax.ShapeDtypeStruct((B, S, HEAD_LANES), BF16),
        scratch_shapes=[pltpu.VMEM((HEAD_LANES // 128, tm, 128), F32), pltpu.VMEM((1, tm, 128), F32)],
        compiler_params=_params(("parallel", "parallel")),
        name="attn_combine",
    )(*os_, *lses, _head_expand_matrix())


SAMPLE_T = 8
_NT = (((1,), (1,)), ((), ()))


def _attn_sample_body(q_ref, n0_ref, n1_ref, n2_ref, c0_ref, c1_ref, c2_ref, attn_ref):
    q = q_ref[0]
    new_refs = (n0_ref, n1_ref, n2_ref)
    cache_refs = (c0_ref, c1_ref, c2_ref)
    hm = _head_masks()
    rows = 2 * SAMPLE_T

    def stride_mask(shape, d, lower):
        t = lax.broadcasted_iota(jnp.int32, shape, 0) % SAMPLE_T
        p = lax.broadcasted_iota(jnp.int32, shape, 1)
        back = (t - p) if lower else (p - t)
        return (back >= 0) & ((back & (d - 1)) == 0)

    cache_ok = [stride_mask((rows, WINDOWS[gi]), DILATIONS[gi], False) for gi in range(N_GROUPS)]
    new_ok = [stride_mask((rows, SAMPLE_T), DILATIONS[gi], True) for gi in range(N_GROUPS)]

    for hp in range(HEADS // 2):
        ch = slice(hp * 128, (hp + 1) * 128)
        scores = []
        for gi in range(N_GROUPS):
            qp = q[:, gi * HEAD_LANES + hp * 128:gi * HEAD_LANES + (hp + 1) * 128]
            lhs = jnp.concatenate([jnp.where(hm[0], qp, 0.0), jnp.where(hm[1], qp, 0.0)], axis=0).astype(BF16)
            kt = cache_refs[gi][0, 0, 2 * hp:2 * hp + 2].reshape(128, WINDOWS[gi]).astype(BF16)
            s = jnp.dot(lhs, kt, preferred_element_type=F32)
            kn = new_refs[gi][0, :, ch].astype(BF16)
            sn = lax.dot_general(lhs, kn, _NT, preferred_element_type=F32)
            scores.append((jnp.where(cache_ok[gi], s, NEG_INF), jnp.where(new_ok[gi], sn, NEG_INF)))
        m = functools.reduce(jnp.maximum, [jnp.max(x, axis=-1, keepdims=True) for pair in scores for x in pair])
        den = jnp.zeros((rows, 1), F32)
        o = jnp.zeros((rows, 128), F32)
        for gi in range(N_GROUPS):
            p = jnp.exp(scores[gi][0] - m)
            pn = jnp.exp(scores[gi][1] - m)
            den = den + jnp.sum(p, axis=-1, keepdims=True) + jnp.sum(pn, axis=-1, keepdims=True)
            vt = cache_refs[gi][0, 1, 2 * hp:2 * hp + 2].reshape(128, WINDOWS[gi]).astype(BF16)
            vn = new_refs[gi][0, :, HEAD_LANES + hp * 128:HEAD_LANES + (hp + 1) * 128].astype(BF16)
            o = o + lax.dot_general(p.astype(BF16), vt, _NT, preferred_element_type=F32)
            o = o + jnp.dot(pn.astype(BF16), vn, preferred_element_type=F32)
        o = o / den
        attn_ref[0, :, ch] = jnp.where(hm[0], o[:SAMPLE_T], o[SAMPLE_T:])


def _attn_sample(q, kv_new, caches_t):
    B = q.shape[0]
    specs = []
    for gi, c in enumerate(caches_t):
        assert c.shape[1:] == (2, HEADS, HEAD_DIM, WINDOWS[gi]), "decode caches must hold a full window"
        specs.append(pl.BlockSpec((1,) + c.shape[1:], lambda b: (b, 0, 0, 0, 0)))
    new_spec = pl.BlockSpec((1, SAMPLE_T, KV_WIDTH), lambda b: (b, 0, 0))
    return pl.pallas_call(
        _attn_sample_body,
        grid=(B,),
        in_specs=[pl.BlockSpec((1, SAMPLE_T, Q_WIDTH), lambda b: (b, 0, 0)), new_spec, new_spec, new_spec, *specs],
        out_specs=pl.BlockSpec((1, SAMPLE_T, HEAD_LANES), lambda b: (b, 0, 0)),
        out_shape=jax.ShapeDtypeStruct((B, SAMPLE_T, HEAD_LANES), F32),
        compiler_params=_params(("parallel",)),
        name="attn_sample",
    )(q, *kv_new, *caches_t)


def _merge_body(hs_ref, attn_ref, ga_ref, gb_ref, x_ref, wa_ref, wb_ref, wp_ref, wo_ref, gf_ref, wq_ref, sk_ref,
                x1_ref, xn_ref, st_ref, *, nb, ts):
    rows = nb * ts
    hs = jnp.concatenate([hs_ref[:, k * D_MODEL:(k + 1) * D_MODEL] for k in range(nb)], axis=0).astype(BF16)
    dot = lambda a, w_ref: jnp.dot(a, w_ref[...], preferred_element_type=F32)
    branch_a = dot(hs, wa_ref) * jax.nn.sigmoid(dot(hs, wb_ref))
    branch_b = dot(attn_ref[...].reshape(rows, HEAD_LANES).astype(BF16), wp_ref)
    ga = ga_ref[...].reshape(rows, D_MODEL)
    gb = gb_ref[...].reshape(rows, D_MODEL)
    mix = jax.nn.sigmoid(ga) * branch_a + jax.nn.sigmoid(gb) * branch_b
    x1 = x_ref[...].reshape(rows, D_MODEL) + dot(mix.astype(BF16), wo_ref)
    x1_ref[...] = x1
    ms = jnp.mean(x1 * x1, axis=-1, keepdims=True)
    xn = ((x1 * lax.rsqrt(ms + EPS)) * gf_ref[...]).astype(BF16)
    xn_ref[...] = xn
    qp = dot(xn, wq_ref).astype(BF16)
    for hp in range(2 * PEER_HEADS):
        st_ref[hp] = lax.dot_general(sk_ref[hp], qp[:, hp * PEER_HALF:(hp + 1) * PEER_HALF], _NT,
                                     preferred_element_type=F32)


def _merge(hs_t, attn, ga, gb, x, wa, wb, wp, wo, g_ffn, wq, sk, *, nb, ts):
    B, S, _ = x.shape
    rows = nb * ts
    T = B * S
    nsb = S // ts
    row = lambda w: pl.BlockSpec((nb, ts, w), lambda b, s: (b, s, 0))
    flat = lambda w: pl.BlockSpec((rows, w), lambda b, s: (b * nsb + s, 0))
    return pl.pallas_call(
        functools.partial(_merge_body, nb=nb, ts=ts),
        grid=(B // nb, nsb),
        in_specs=[pl.BlockSpec((ts, nb * D_MODEL), lambda b, s: (s, b)), row(HEAD_LANES), row(D_MODEL),
                  row(D_MODEL), row(D_MODEL),
                  _const_spec((D_MODEL, D_MODEL)), _const_spec((D_MODEL, D_MODEL)),
                  _const_spec((HEAD_LANES, D_MODEL)), _const_spec((D_MODEL, D_MODEL)),
                  _const_spec((1, D_MODEL)), _const_spec((D_MODEL, 2 * PEER_HEADS * PEER_HALF)),
                  _const_spec((2 * PEER_HEADS, N_KEYS, PEER_HALF))],
        out_specs=(flat(D_MODEL), flat(D_MODEL),
                   pl.BlockSpec((2 * PEER_HEADS, N_KEYS, rows), lambda b, s: (0, 0, b * nsb + s))),
        out_shape=(jax.ShapeDtypeStruct((T, D_MODEL), F32), jax.ShapeDtypeStruct((T, D_MODEL), BF16),
                   jax.ShapeDtypeStruct((2 * PEER_HEADS, N_KEYS, T), F32)),
        compiler_params=_params(("parallel", "parallel")),
        name="merge",
    )(hs_t, attn, ga, gb, x, wa, wb, wp, wo, g_ffn.reshape(1, D_MODEL), wq, sk)


ROUTE_LANES = 128
_PAIR_SLABS = ((0, 0, 0, 1, 8), (0, 0, 8, 1, 8), (1, 0, 0, 1, 8), (2, 0, 0, 1, 5), (3, 0, 0, 1, 4),
               (4, 0, 0, 1, 3), (5, 0, 0, 1, 2), (6, 0, 0, 1, 2), (7, 0, 0, 1, 2), (8, 1, 0, 0, 8))


def _top16(s, break_ties):
    idx = lax.broadcasted_iota(jnp.int32, s.shape, 0).astype(F32)
    cur = s
    rank = jnp.full(s.shape, float(PEER_TOPK), F32)
    vals = []
    for r in range(PEER_TOPK):
        m = jnp.max(cur, axis=0, keepdims=True)
        sel = cur == m
        if break_ties:
            sel = idx == jnp.min(jnp.where(sel, idx, float(N_KEYS)), axis=0, keepdims=True)
        rank = jnp.where(sel, float(r), rank)
        cur = jnp.where(sel, NEG_INF, cur)
        vals.append(m)
    n_ranked = jnp.sum(jnp.where(rank < float(PEER_TOPK), 1.0, 0.0), axis=0, keepdims=True)
    return rank, vals, n_ranked


def _route_head(s0, s1, break_ties):
    sub = lax.broadcasted_iota(jnp.int32, (8, ROUTE_LANES), 0)
    rank0, v0, n0 = _top16(s0, break_ties)
    rank1, v1, n1 = _top16(s1, break_ties)
    v0b = jnp.concatenate(v0[8:], axis=0)
    v1a, v1b = jnp.concatenate(v1[:8], axis=0), jnp.concatenate(v1[8:], axis=0)

    cands, cidx = [], []
    for (r0, r0s, r1, r1s, nrow) in _PAIR_SLABS:
        a = (v0b if r0 == 8 else v0[r0])
        b = (v1[0] if r1s == 0 else (v1a if r1 == 0 else v1b))
        c = a + b
        if nrow < 8:
            c = jnp.where(sub < nrow, c, NEG_INF)
        cands.append(c)
        cidx.append(((r0 + r0s * sub) * PEER_TOPK + (r1 + r1s * sub)).astype(F32))

    cur = list(cands)
    picked = [jnp.zeros((8, ROUTE_LANES), F32) for _ in cands]
    big = float(PEER_TOPK * PEER_TOPK)
    for _ in range(PEER_TOPK):
        m = jnp.max(functools.reduce(jnp.maximum, cur), axis=0, keepdims=True)
        sels = [c == m for c in cur]
        if break_ties:
            first = functools.reduce(jnp.minimum, [jnp.where(s, ci, big) for s, ci in zip(sels, cidx)])
            first = jnp.min(first, axis=0, keepdims=True)
            sels = [ci == first for ci in cidx]
        for k, sel in enumerate(sels):
            picked[k] = jnp.where(sel, 1.0, picked[k])
            cur[k] = jnp.where(sel, NEG_INF, cur[k])

    total = functools.reduce(lambda x, y: x + y, picked)
    n_pairs = jnp.sum(total, axis=0, keepdims=True)
    top = v0[0] + v1[0]
    z = functools.reduce(lambda x, y: x + y,
                         [jnp.sum(pk * jnp.exp(c - top), axis=0, keepdims=True) for pk, c in zip(picked, cands)])
    counts = [jnp.sum(picked[0] + picked[1], axis=0, keepdims=True)]
    counts += [jnp.sum(picked[k], axis=0, keepdims=True) for k in range(2, 9)]
    counts += [picked[9][i:i + 1, :] for i in range(8)]
    l0 = jnp.zeros((N_KEYS, ROUTE_LANES), F32)
    for r in range(PEER_TOPK):
        l0 = jnp.where(rank0 == float(r), counts[r], l0)

    k16 = float(PEER_TOPK)
    n_bad = jnp.abs(n0 - k16) + jnp.abs(n1 - k16) + jnp.abs(n_pairs - k16)
    return jnp.exp(s0 - v0[0]) / z, l0, jnp.exp(s1 - v1[0]), rank1, n_bad


ROUTE_HEADS_PER_ITER = 4


def _bf16_pair_words(x):
    w = lax.bitcast_convert_type(x.astype(BF16).astype(F32), jnp.uint32)
    return w | (w >> 16)


def _route_body(st_ref, c0_ref, l0_ref, b1_ref, r1_ref):
    def heads(it, carry):
        def run(break_ties):
            n_bad = None
            for k in range(ROUTE_HEADS_PER_ITER):
                h = it * ROUTE_HEADS_PER_ITER + k
                c0, l0, b1, r1, bad = _route_head(st_ref[2 * h], st_ref[2 * h + 1], break_ties)
                c0_ref[h] = _bf16_pair_words(c0)
                l0_ref[h] = _bf16_pair_words(l0)
                b1_ref[h] = b1.astype(BF16)
                r1_ref[h] = r1.astype(BF16)
                n_bad = bad if n_bad is None else n_bad + bad
            return jnp.max(n_bad)

        n_bad = run(False)

        @pl.when(n_bad > 0.0)
        def _():
            run(True)

        return carry

    lax.fori_loop(0, PEER_HEADS // ROUTE_HEADS_PER_ITER, heads, 0)


def _route(st):
    T = st.shape[2]
    out = lambda dt: jax.ShapeDtypeStruct((PEER_HEADS, N_KEYS, T), dt)
    spec = pl.BlockSpec((PEER_HEADS, N_KEYS, ROUTE_LANES), lambda i: (0, 0, i))
    return pl.pallas_call(
        _route_body,
        grid=(T // ROUTE_LANES,),
        in_specs=[pl.BlockSpec((2 * PEER_HEADS, N_KEYS, ROUTE_LANES), lambda i: (0, 0, i))],
        out_specs=(spec, spec, spec, spec),
        out_shape=(out(jnp.uint32), out(jnp.uint32), out(BF16), out(BF16)),
        compiler_params=_params(("parallel",)),
        name="peer_route",
    )(st)


PEER_EXPERT_BLOCK = 1024
PEER_I_PER_BLOCK = PEER_EXPERT_BLOCK // N_KEYS
GELU_C = math.sqrt(2.0 / math.pi)


def _peer_body(xn_ref, u_ref, vt_ref, c0_ref, l0_ref, b1_ref, r1_ref, x1_ref, gfin_ref, y_ref, acc, pg):
    e = pl.program_id(1)

    @pl.when(e == 0)
    def _():
        acc[...] = jnp.zeros_like(acc)

    tt = xn_ref.shape[0]
    tile = (N_KEYS // BF16_ROWS, BF16_ROWS, tt)
    zero = jnp.zeros(tile, BF16)

    def row16(ref, h, ib):
        return pltpu.bitcast(jnp.broadcast_to(ref[h, ib:ib + 1, :], (8, tt)), BF16)

    def gated(ib, a):
        g = None
        for h in range(PEER_HEADS):
            term = jnp.where(r1_ref[h].reshape(tile) < row16(l0_ref, h, ib)[None], b1_ref[h].reshape(tile), zero)
            term = term * row16(c0_ref, h, ib)[None]
            g = term if g is None else g + term
        x = a.astype(BF16).reshape(tile)
        z = (x * x * (GELU_C * 0.044715) + GELU_C) * x
        hx = 0.5 * x
        return (g * (hx + hx * jnp.tanh(z))).reshape(N_KEYS, tt)

    act = lax.dot_general(u_ref[...], xn_ref[...], _NT, preferred_element_type=F32)
    for ib in range(PEER_I_PER_BLOCK):
        rows = slice(ib * N_KEYS, (ib + 1) * N_KEYS)
        pg[rows, :] = gated(ib, act[rows, :])
    acc[...] += jnp.dot(vt_ref[...], pg[...], preferred_element_type=F32)

    @pl.when(e == pl.num_programs(1) - 1)
    def _():
        x2 = x1_ref[...] + acc[...].T
        ms = jnp.mean(x2 * x2, axis=-1, keepdims=True)
        y_ref[...] = (x2 * lax.rsqrt(ms + EPS)) * gfin_ref[...]


def _peer(xn, u_bf, vt_bf, c0, l0, b1, r1, x1, g_final, *, tt):
    T = xn.shape[0]
    tok = pl.BlockSpec((tt, D_MODEL), lambda t, e: (t, 0))
    rt = pl.BlockSpec((PEER_HEADS, N_KEYS, tt), lambda t, e: (0, 0, t))
    rows = pl.BlockSpec((PEER_HEADS, PEER_I_PER_BLOCK, tt), lambda t, e: (0, e, t))
    return pl.pallas_call(
        _peer_body,
        grid=(T // tt, N_EXPERTS // PEER_EXPERT_BLOCK),
        in_specs=[tok, pl.BlockSpec((PEER_EXPERT_BLOCK, D_MODEL), lambda t, e: (e, 0)),
                  pl.BlockSpec((D_MODEL, PEER_EXPERT_BLOCK), lambda t, e: (0, e)),
                  rows, rows, rt, rt, tok, _const_spec((1, D_MODEL))],
        out_specs=tok,
        out_shape=jax.ShapeDtypeStruct((T, D_MODEL), F32),
        scratch_shapes=[pltpu.VMEM((D_MODEL, tt), F32), pltpu.VMEM((PEER_EXPERT_BLOCK, tt), BF16)],
        compiler_params=_params(("parallel", "arbitrary")),
        name="peer_dense",
    )(xn, u_bf, vt_bf, c0, l0, b1, r1, x1, g_final.reshape(1, D_MODEL))


def _pack_w_in(w_in):
    o_k = D_MODEL + Q_WIDTH
    o_v = o_k + Q_WIDTH
    o_g = o_v + Q_WIDTH
    parts = [w_in[:, :o_k]]
    for gi in range(N_GROUPS):
        parts.append(w_in[:, o_k + gi * HEAD_LANES:o_k + (gi + 1) * HEAD_LANES])
        parts.append(w_in[:, o_v + gi * HEAD_LANES:o_v + (gi + 1) * HEAD_LANES])
    parts.append(w_in[:, o_g:])
    return jnp.concatenate(parts, axis=1).astype(BF16)


def _layer(x, h0_re, h0_im, caches, wts, *, nb, ts, ssm_ts, peer_tt):
    B, S, _ = x.shape
    proj = _in_proj(x, wts["g_mix"], wts["w_in"], nb=nb, ts=ts, dilated=caches is None)
    u_t, kvs, (ga, gb) = proj[0], proj[-5:-2], proj[-2:]
    hs_t, hre, him = _ssm(u_t, h0_re, h0_im, *wts["ssm"], ts=ssm_ts)
    if caches is None:
        outs = [_attn_prompt(proj[1 + gi], proj[1 + N_GROUPS + gi], gi) for gi in range(N_GROUPS)]
        attn = _combine([o for o, _ in outs], [l for _, l in outs], tm=2 * ts)
    else:
        attn = _attn_sample(proj[1], kvs, caches)
    x1, xn, st = _merge(hs_t, attn, ga, gb, x, wts["w_glu_a"], wts["w_glu_b"], wts["w_attn_proj"],
                        wts["w_out"], wts["g_ffn"], wts["w_qp"], wts["sub_keys"], nb=nb, ts=ts)
    c0, l0, b1, r1 = _route(st)
    y = _peer(xn, wts["u_tab"], wts["v_tab_t"], c0, l0, b1, r1, x1, wts["g_final"], tt=peer_tt)
    return y.reshape(B, S, D_MODEL), hre, him, kvs


def kernel(x_prompt, x_sample, state_ssm_re, state_ssm_im, cache_kv_w128, cache_kv_w512, cache_kv_w2048, g_mix, w_in, lam_re, lam_im, log_dt, b_re, b_im, c_re, c_im, d_skip, w_glu_a, w_glu_b, w_attn_proj, w_out, g_ffn, w_qp, sub_keys, u_tab, v_tab, g_final):
    assert w_in.shape[0] == 1, "single-layer model"
    wts = {
        "g_mix": g_mix[0],
        "w_in": _pack_w_in(w_in[0]),
        "ssm": _ssm_weights(lam_re[0], lam_im[0], log_dt[0], b_re[0], b_im[0], c_re[0], c_im[0], d_skip[0]),
        "w_glu_a": w_glu_a[0].astype(BF16),
        "w_glu_b": w_glu_b[0].astype(BF16),
        "w_attn_proj": w_attn_proj[0].astype(BF16),
        "w_out": w_out[0].astype(BF16),
        "g_ffn": g_ffn[0],
        "w_qp": w_qp[0].astype(BF16),
        "sub_keys": sub_keys[0].reshape(2 * PEER_HEADS, N_KEYS, PEER_HALF).astype(BF16),
        "u_tab": u_tab[0].astype(BF16),
        "v_tab_t": v_tab[0].astype(BF16).T,
        "g_final": g_final,
    }
    Bp, Sp, _ = x_prompt.shape
    Bs, Ss, _ = x_sample.shape
    assert Ss == SAMPLE_T
    zeros = jnp.zeros((Bp, SSM_LANES), F32)
    yp, hre_p, him_p, kv_p = _layer(x_prompt, zeros, zeros, None, wts, nb=1, ts=256, ssm_ts=32, peer_tt=1024)
    caches = tuple(c[0].transpose(0, 2, 3, 4, 1) for c in (cache_kv_w128, cache_kv_w512, cache_kv_w2048))
    ys, hre_s, him_s, kv_s = _layer(x_sample, state_ssm_re[0].reshape(Bs, SSM_LANES),
                                    state_ssm_im[0].reshape(Bs, SSM_LANES), caches, wts,
                                    nb=16, ts=SAMPLE_T, ssm_ts=SAMPLE_T, peer_tt=1024)

    state = lambda h: h.reshape(1, -1, SSM_GROUPS, SSM_STATE)
    kv_rows = lambda kv, w: kv[:, kv.shape[1] - min(w, kv.shape[1]):].reshape(1, kv.shape[0], -1, 2, HEADS, HEAD_DIM)
    return (yp, ys, state(hre_p), state(him_p),
            kv_rows(kv_p[0], WINDOWS[0]), kv_rows(kv_p[1], WINDOWS[1]), kv_rows(kv_p[2], WINDOWS[2]),
            state(hre_s), state(him_s),
            kv_rows(kv_s[0], SAMPLE_T), kv_rows(kv_s[1], SAMPLE_T), kv_rows(kv_s[2], SAMPLE_T))
```

```python
import functools
import math

import jax
import jax.numpy as jnp
from jax import lax
from jax.experimental import pallas as pl
from jax.experimental.pallas import tpu as pltpu

F32 = jnp.float32
BF16 = jnp.bfloat16

D_MODEL = 1024
SSM_CH = 16
SSM_GROUPS = D_MODEL // SSM_CH
SSM_STATE = 64
SSM_LANES = SSM_GROUPS * SSM_STATE
HEAD_DIM = 64
HEADS = 8
HEAD_LANES = HEADS * HEAD_DIM
N_GROUPS = 3
WINDOWS = (128, 512, 2048)
DILATIONS = (1, 4, 16)
N_STEPS = 128
Q_WIDTH = N_GROUPS * HEAD_LANES
KV_WIDTH = 2 * HEAD_LANES
PROJ_WIDTH = D_MODEL + 3 * Q_WIDTH + 2 * D_MODEL
PEER_HEADS = 8
PEER_TOPK = 16
N_KEYS = 128
N_EXPERTS = N_KEYS * N_KEYS
PEER_HALF = 128
EPS = 1e-6
NEG_INF = float("-inf")

SSM_BATCH_TILE = 16
SSM_BLOCK_GROUPS = 8
SSM_N_BLOCKS = SSM_GROUPS // SSM_BLOCK_GROUPS
SSM_BLOCK_STATE = SSM_BLOCK_GROUPS * SSM_STATE
SCAN_LANES = 512

VMEM_LIMIT_BYTES = 56 * 1024 * 1024
BF16_ROWS = 16


def _params(semantics, flags=None):
    return pltpu.CompilerParams(dimension_semantics=semantics, vmem_limit_bytes=VMEM_LIMIT_BYTES, flags=flags)


def _const_spec(shape):
    zeros = (0,) * len(shape)
    return pl.BlockSpec(shape, lambda *_: zeros)


def _gelu(x):
    return jax.nn.gelu(x, approximate=True)


_SEG_U = (0, D_MODEL)
_SEG_Q = (D_MODEL, D_MODEL + Q_WIDTH)
_SEG_KV0 = D_MODEL + Q_WIDTH
_SEG_GA = _SEG_KV0 + N_GROUPS * KV_WIDTH
_SEG_GB = _SEG_GA + D_MODEL


LANE_SLABS = D_MODEL // 128


def _in_proj_body(x_ref, g_ref, w_ref, u_ref, *refs, nb, ts, dilated):
    x = x_ref[...].reshape(nb * ts, D_MODEL)
    ms = jnp.mean(x * x, axis=-1, keepdims=True)
    xn = (x * lax.rsqrt(ms + EPS)) * g_ref[...]
    xb = xn.astype(BF16)
    scale = HEAD_DIM ** -0.5

    def seg(lhs, a, b):
        return jnp.dot(lhs, w_ref[:, a:b], preferred_element_type=F32)

    u = seg(xb, *_SEG_U)
    for k in range(nb):
        u_ref[:, k * D_MODEL:(k + 1) * D_MODEL] = u[k * ts:(k + 1) * ts]
    kv_cols = lambda gi: (_SEG_KV0 + gi * KV_WIDTH, _SEG_KV0 + (gi + 1) * KV_WIDTH)
    q_cols = lambda gi: (D_MODEL + gi * HEAD_LANES, D_MODEL + (gi + 1) * HEAD_LANES)

    if not dilated:
        q_ref, kv_refs, (ga_ref, gb_ref) = refs[0], refs[1:4], refs[4:6]
        q_ref[...] = (seg(xb, *_SEG_Q) * scale).reshape(nb, ts, Q_WIDTH)
        for gi in range(N_GROUPS):
            kv_refs[gi][...] = seg(xb, *kv_cols(gi)).reshape(nb, ts, KV_WIDTH)
    else:
        qd_refs, kvd_refs, kv_refs, (ga_ref, gb_ref), (xs, ys) = refs[0:3], refs[3:6], refs[6:9], refs[9:11], refs[11:13]
        for c in range(LANE_SLABS):
            xs[c] = xn[:, c * 128:(c + 1) * 128]
        for gi in range(N_GROUPS):
            d = DILATIONS[gi]
            n = ts // d
            if d == 1:
                lhs = xb
            else:
                lhs = jnp.concatenate(
                    [jnp.concatenate([xs[c, pl.ds(r, n, stride=d), :] for c in range(LANE_SLABS)], axis=1)
                     for r in range(d)], axis=0).astype(BF16)
            qd_refs[gi][0] = (seg(lhs, *q_cols(gi)) * scale).reshape(d, n, HEAD_LANES).astype(BF16)
            kv = seg(lhs, *kv_cols(gi))
            kvd_refs[gi][0] = kv.reshape(d, n, KV_WIDTH).astype(BF16)
            if d > 1:
                for r in range(d):
                    for c in range(LANE_SLABS):
                        ys[c, pl.ds(r, n, stride=d), :] = kv[r * n:(r + 1) * n, c * 128:(c + 1) * 128]
                kv = jnp.concatenate([ys[c] for c in range(LANE_SLABS)], axis=1)
            kv_refs[gi][0] = kv[ts - kv_refs[gi].shape[1]:]
    ga_ref[...] = seg(xb, _SEG_GA, _SEG_GB).reshape(nb, ts, D_MODEL)
    gb_ref[...] = seg(xb, _SEG_GB, PROJ_WIDTH).reshape(nb, ts, D_MODEL)


def _in_proj(x, g_mix, w_perm, *, nb, ts, dilated):
    B, S, _ = x.shape
    assert not dilated or nb == 1
    row = lambda w: pl.BlockSpec((nb, ts, w), lambda b, s: (b, s, 0))
    f32_rows = lambda w: jax.ShapeDtypeStruct((B, S, w), F32)
    gates = [(row(D_MODEL), f32_rows(D_MODEL))] * 2
    if dilated:
        def regrouped(w):
            return [(pl.BlockSpec((1, d, ts // d, w), lambda b, s: (b, 0, s, 0)),
                     jax.ShapeDtypeStruct((B, d, S // d, w), BF16)) for d in DILATIONS]

        def window_rows(gi):
            tail = min(WINDOWS[gi], S)
            if tail < ts:
                assert ts % tail == 0
                spec = pl.BlockSpec((1, tail, KV_WIDTH), lambda b, s: (b, 0, 0))
            else:
                assert tail % ts == 0
                first = (S - tail) // ts
                spec = pl.BlockSpec((1, ts, KV_WIDTH), lambda b, s: (b, jnp.maximum(s - first, 0), 0))
            return spec, jax.ShapeDtypeStruct((B, tail, KV_WIDTH), F32)

        outs = regrouped(HEAD_LANES) + regrouped(KV_WIDTH) + [window_rows(gi) for gi in range(N_GROUPS)] + gates
        scratch = [pltpu.VMEM((LANE_SLABS, ts, 128), F32), pltpu.VMEM((LANE_SLABS, ts, 128), F32)]
    else:
        outs = [(row(Q_WIDTH), f32_rows(Q_WIDTH))] + [(row(KV_WIDTH), f32_rows(KV_WIDTH))] * N_GROUPS + gates
        scratch = []
    outs = [(pl.BlockSpec((ts, nb * D_MODEL), lambda b, s: (s, b)), jax.ShapeDtypeStruct((S, B * D_MODEL), F32))] + outs
    return pl.pallas_call(
        functools.partial(_in_proj_body, nb=nb, ts=ts, dilated=dilated),
        grid=(B // nb, S // ts),
        in_specs=[row(D_MODEL), _const_spec((1, D_MODEL)),
                  pl.BlockSpec((D_MODEL, PROJ_WIDTH), lambda b, s: (0, 0), pipeline_mode=pl.Buffered(1))],
        out_specs=tuple(spec for spec, _ in outs),
        out_shape=tuple(shape for _, shape in outs),
        scratch_shapes=scratch,
        compiler_params=_params(("parallel", "arbitrary")),
        name="in_proj",
    )(x, g_mix.reshape(1, D_MODEL), w_perm)


def _ssm_body(u_ref, h0re_ref, h0im_ref, wx_ref, cy_ref, are_ref, aim_ref, dsk_ref,
              hs_ref, hre_ref, him_ref, xre, xim, st_re, st_im, *, ts):
    si = pl.program_id(1)
    rows = ts * SSM_BATCH_TILE

    @pl.when(si == 0)
    def _():
        st_re[...] = h0re_ref[...]
        st_im[...] = h0im_ref[...]

    u = u_ref[...].reshape(rows, D_MODEL)
    ub = u.astype(BF16)
    for j in range(SSM_N_BLOCKS):
        r = jnp.dot(ub[:, j * 128:(j + 1) * 128], wx_ref[j], preferred_element_type=F32)
        cols = slice(j * SSM_BLOCK_STATE, (j + 1) * SSM_BLOCK_STATE)
        xre[:, cols] = r[:, :SSM_BLOCK_STATE]
        xim[:, cols] = r[:, SSM_BLOCK_STATE:]

    for c in range(SSM_LANES // SCAN_LANES):
        cols = slice(c * SCAN_LANES, (c + 1) * SCAN_LANES)
        a_re = jnp.broadcast_to(are_ref[:, cols], (SSM_BATCH_TILE, SCAN_LANES))
        a_im = jnp.broadcast_to(aim_ref[:, cols], (SSM_BATCH_TILE, SCAN_LANES))

        def step(t, carry, cols=cols, a_re=a_re, a_im=a_im):
            h_re, h_im = carry
            r0 = pl.multiple_of(t * SSM_BATCH_TILE, SSM_BATCH_TILE)
            n_re = a_re * h_re - a_im * h_im + xre[pl.ds(r0, SSM_BATCH_TILE), cols]
            n_im = a_re * h_im + a_im * h_re + xim[pl.ds(r0, SSM_BATCH_TILE), cols]
            xre[pl.ds(r0, SSM_BATCH_TILE), cols] = n_re
            xim[pl.ds(r0, SSM_BATCH_TILE), cols] = n_im
            return n_re, n_im

        h_re, h_im = lax.fori_loop(0, ts, step, (st_re[:, cols], st_im[:, cols]), unroll=min(ts, 8))
        st_re[:, cols] = h_re
        st_im[:, cols] = h_im

    for j in range(SSM_N_BLOCKS):
        cols = slice(j * SSM_BLOCK_STATE, (j + 1) * SSM_BLOCK_STATE)
        h = jnp.concatenate([xre[:, cols], xim[:, cols]], axis=1).astype(BF16)
        y = jnp.dot(h, cy_ref[j], preferred_element_type=F32)
        ch = slice(j * 128, (j + 1) * 128)
        y = y + dsk_ref[:, ch] * u[:, ch]
        hs_ref[:, :, ch] = _gelu(y).reshape(ts, SSM_BATCH_TILE, 128)

    @pl.when(si == pl.num_programs(1) - 1)
    def _():
        hre_ref[...] = st_re[...]
        him_ref[...] = st_im[...]


def _ssm(u_t, h0_re, h0_im, wx, cy, a_re, a_im, dsk, *, ts):
    S = u_t.shape[0]
    B = u_t.shape[1] // D_MODEL
    nbt = B // SSM_BATCH_TILE
    u4 = u_t.reshape(S, nbt, SSM_BATCH_TILE, D_MODEL)
    tile = pl.BlockSpec((ts, None, SSM_BATCH_TILE, D_MODEL), lambda b, s: (s, b, 0, 0))
    state = pl.BlockSpec((SSM_BATCH_TILE, SSM_LANES), lambda b, s: (b, 0))
    rows = ts * SSM_BATCH_TILE
    hs, hre, him = pl.pallas_call(
        functools.partial(_ssm_body, ts=ts),
        grid=(nbt, S // ts),
        in_specs=[tile, state, state,
                  _const_spec((SSM_N_BLOCKS, 128, 2 * SSM_BLOCK_STATE)),
                  _const_spec((SSM_N_BLOCKS, 2 * SSM_BLOCK_STATE, 128)),
                  _const_spec((1, SSM_LANES)), _const_spec((1, SSM_LANES)), _const_spec((1, D_MODEL))],
        out_specs=(tile, state, state),
        out_shape=(jax.ShapeDtypeStruct(u4.shape, F32),
                   jax.ShapeDtypeStruct((B, SSM_LANES), F32), jax.ShapeDtypeStruct((B, SSM_LANES), F32)),
        scratch_shapes=[pltpu.VMEM((rows, SSM_LANES), F32), pltpu.VMEM((rows, SSM_LANES), F32),
                        pltpu.VMEM((SSM_BATCH_TILE, SSM_LANES), F32), pltpu.VMEM((SSM_BATCH_TILE, SSM_LANES), F32)],
        compiler_params=_params(("parallel", "arbitrary")),
        name="ssm",
    )(u4, h0_re, h0_im, wx, cy, a_re, a_im, dsk)
    return hs.reshape(S, B * D_MODEL), hre, him


def _ssm_weights(lam_re, lam_im, log_dt, b_re, b_im, c_re, c_im, d_skip):
    dt = jnp.exp(log_dt)[:, None]
    mag = jnp.exp(lam_re * dt)
    ang = lam_im * dt
    ab_re, ab_im = mag * jnp.cos(ang), mag * jnp.sin(ang)
    den = lam_re * lam_re + lam_im * lam_im
    f_re = ((ab_re - 1.0) * lam_re + ab_im * lam_im) / den
    f_im = (ab_im * lam_re - (ab_re - 1.0) * lam_im) / den
    bb_re = f_re[..., None] * b_re - f_im[..., None] * b_im
    bb_im = f_re[..., None] * b_im + f_im[..., None] * b_re
    eye = jnp.eye(SSM_BLOCK_GROUPS, dtype=F32)

    def in_block(bb):
        bb = bb.reshape(SSM_N_BLOCKS, SSM_BLOCK_GROUPS, SSM_STATE, SSM_CH)
        return jnp.einsum('jgnc,gh->jgchn', bb, eye).reshape(SSM_N_BLOCKS, 128, SSM_BLOCK_STATE)

    def out_block(cc):
        cc = cc.reshape(SSM_N_BLOCKS, SSM_BLOCK_GROUPS, SSM_CH, SSM_STATE)
        return jnp.einsum('jgcn,gh->jgnhc', cc, eye).reshape(SSM_N_BLOCKS, SSM_BLOCK_STATE, 128)

    wx = jnp.concatenate([in_block(bb_re), in_block(bb_im)], axis=2).astype(BF16)
    cy = jnp.concatenate([out_block(c_re), out_block(-c_im)], axis=1).astype(BF16)
    return (wx, cy, ab_re.reshape(1, SSM_LANES), ab_im.reshape(1, SSM_LANES), d_skip.reshape(1, D_MODEL))


def _head_masks():
    lane = lax.broadcasted_iota(jnp.int32, (1, 128), 1)
    return [(lane // HEAD_DIM) == hh for hh in range(2)]


ATTN_UNROLL = 4


def _attn_body(q_ref, kv_ref, o_ref, lse_ref, *, nblk):
    d = q_ref.shape[1]
    window = min(2, nblk) * N_STEPS
    qi = lax.broadcasted_iota(jnp.int32, (2 * N_STEPS, window), 0) % N_STEPS
    ki = lax.broadcasted_iota(jnp.int32, (2 * N_STEPS, window), 1)
    hm = _head_masks()
    lane = lax.broadcasted_iota(jnp.int32, (1, 128), 1)

    def tile(t, carry):
        r = t // nblk
        n = t % nblk
        q0 = pl.multiple_of(n * N_STEPS, N_STEPS)
        k0 = pl.multiple_of(jnp.maximum(n - 1, 0) * N_STEPS, N_STEPS)
        q = q_ref[0, r, pl.ds(q0, N_STEPS), :]
        kv = kv_ref[0, r, pl.ds(k0, window), :]
        back = (q0 - k0) + qi - ki
        mask = (back >= 0) & (back <= N_STEPS)
        lse_tile = jnp.zeros((N_STEPS, 128), F32)
        for hp in range(HEADS // 2):
            ch = slice(hp * 128, (hp + 1) * 128)
            qp, kp, vp = q[:, ch], kv[:, ch], kv[:, HEAD_LANES + hp * 128:HEAD_LANES + (hp + 1) * 128]
            zeros = jnp.zeros_like(qp)
            q2 = jnp.concatenate([jnp.where(hm[0], qp, zeros), jnp.where(hm[1], qp, zeros)], axis=0)
            s = lax.dot_general(q2, kp, _NT, preferred_element_type=F32)
            s = jnp.where(mask, s, NEG_INF)
            m = jnp.max(s, axis=-1, keepdims=True)
            p = jnp.exp(s - m)
            den = jnp.sum(p, axis=-1, keepdims=True)
            o2 = jnp.dot((p / den).astype(BF16), vp, preferred_element_type=F32)
            lse = m + jnp.log(den)
            for hh in range(2):
                lse_tile = jnp.where((lane // 16) == (2 * hp + hh), lse[hh * N_STEPS:(hh + 1) * N_STEPS], lse_tile)
            o_ref[0, r, pl.ds(q0, N_STEPS), ch] = jnp.where(hm[0], o2[:N_STEPS], o2[N_STEPS:])
        lse_ref[0, r, pl.ds(q0, N_STEPS), :] = lse_tile
        return carry

    lax.fori_loop(0, d * nblk, tile, 0, unroll=ATTN_UNROLL)


def _attn_prompt(q, kv, gi):
    B, d, L, _ = q.shape
    blk = lambda w: pl.BlockSpec((1, d, L, w), lambda b: (b, 0, 0, 0))
    return pl.pallas_call(
        functools.partial(_attn_body, nblk=L // N_STEPS),
        grid=(B,),
        in_specs=[blk(HEAD_LANES), blk(KV_WIDTH)],
        out_specs=(blk(HEAD_LANES), blk(128)),
        out_shape=(jax.ShapeDtypeStruct((B, d, L, HEAD_LANES), F32),
                   jax.ShapeDtypeStruct((B, d, L, 128), F32)),
        compiler_params=_params(("parallel",)),
        name=f"attn_prompt_g{gi}",
    )(q, kv)


def _combine_body(o0_ref, o1_ref, o2_ref, l0_ref, l1_ref, l2_ref, ex_ref, attn_ref, oslab, lslab, *, tm):
    def natural(ref, gi, slab):
        d = DILATIONS[gi]
        if d == 1:
            return ref[0, 0]
        n = tm // d
        nslab = ref.shape[-1] // 128
        for r in range(d):
            for c in range(nslab):
                slab[c, pl.ds(r, n, stride=d), :] = ref[0, r, :, c * 128:(c + 1) * 128]
        return jnp.concatenate([slab[c] for c in range(nslab)], axis=1)

    o_refs, l_refs = (o0_ref, o1_ref, o2_ref), (l0_ref, l1_ref, l2_ref)
    ls = [natural(l_refs[gi], gi, lslab) for gi in range(N_GROUPS)]
    m = jnp.maximum(jnp.maximum(ls[0], ls[1]), ls[2])
    es = [jnp.exp(l - m) for l in ls]
    den = es[0] + es[1] + es[2]
    acc = None
    for gi in range(N_GROUPS):
        w = es[gi] / den
        w_hi = w.astype(BF16)
        w_lo = (w - w_hi.astype(F32)).astype(BF16)
        wx = (jnp.dot(w_hi, ex_ref[...], preferred_element_type=F32)
              + jnp.dot(w_lo, ex_ref[...], preferred_element_type=F32))
        term = wx * natural(o_refs[gi], gi, oslab)
        acc = term if acc is None else acc + term
    attn_ref[0] = acc.astype(BF16)


def _head_expand_matrix():
    src = jnp.arange(128)[:, None]
    dst = jnp.arange(HEAD_LANES)[None, :]
    return (src == (dst // HEAD_DIM) * 16).astype(BF16)


def _combine(os_, lses, *, tm):
    B = os_[0].shape[0]
    S = os_[0].shape[1] * os_[0].shape[2]
    blk = lambda d, w: pl.BlockSpec((1, d, tm // d, w), lambda b, s: (b, 0, s, 0))
    return pl.pallas_call(
        functools.partial(_combine_body, tm=tm),
        grid=(B, S // tm),
        in_specs=[blk(d, HEAD_LANES) for d in DILATIONS] + [blk(d, 128) for d in DILATIONS]
                 + [_const_spec((128, HEAD_LANES))],
        out_specs=pl.BlockSpec((1, tm, HEAD_LANES), lambda b, s: (b, s, 0)),
        out_shape=jax.ShapeDtypeStruct((B, S, HEAD_LANES), BF16),
        scratch_shapes=[pltpu.VMEM((HEAD_LANES // 128, tm, 128), F32), pltpu.VMEM((1, tm, 128), F32)],
        compiler_params=_params(("parallel", "parallel")),
        name="attn_combine",
    )(*os_, *lses, _head_expand_matrix())


SAMPLE_T = 8
_NT = (((1,), (1,)), ((), ()))


def _attn_sample_body(q_ref, n0_ref, n1_ref, n2_ref, c0_ref, c1_ref, c2_ref, attn_ref):
    q = q_ref[0]
    new_refs = (n0_ref, n1_ref, n2_ref)
    cache_refs = (c0_ref, c1_ref, c2_ref)
    hm = _head_masks()
    rows = 2 * SAMPLE_T

    def stride_mask(shape, d, lower):
        t = lax.broadcasted_iota(jnp.int32, shape, 0) % SAMPLE_T
        p = lax.broadcasted_iota(jnp.int32, shape, 1)
        back = (t - p) if lower else (p - t)
        return (back >= 0) & ((back & (d - 1)) == 0)

    cache_ok = [stride_mask((rows, WINDOWS[gi]), DILATIONS[gi], False) for gi in range(N_GROUPS)]
    new_ok = [stride_mask((rows, SAMPLE_T), DILATIONS[gi], True) for gi in range(N_GROUPS)]

    for hp in range(HEADS // 2):
        ch = slice(hp * 128, (hp + 1) * 128)
        scores = []
        for gi in range(N_GROUPS):
            qp = q[:, gi * HEAD_LANES + hp * 128:gi * HEAD_LANES + (hp + 1) * 128]
            lhs = jnp.concatenate([jnp.where(hm[0], qp, 0.0), jnp.where(hm[1], qp, 0.0)], axis=0).astype(BF16)
            kt = cache_refs[gi][0, 0, 2 * hp:2 * hp + 2].reshape(128, WINDOWS[gi]).astype(BF16)
            s = jnp.dot(lhs, kt, preferred_element_type=F32)
            kn = new_refs[gi][0, :, ch].astype(BF16)
            sn = lax.dot_general(lhs, kn, _NT, preferred_element_type=F32)
            scores.append((jnp.where(cache_ok[gi], s, NEG_INF), jnp.where(new_ok[gi], sn, NEG_INF)))
        m = functools.reduce(jnp.maximum, [jnp.max(x, axis=-1, keepdims=True) for pair in scores for x in pair])
        den = jnp.zeros((rows, 1), F32)
        o = jnp.zeros((rows, 128), F32)
        for gi in range(N_GROUPS):
            p = jnp.exp(scores[gi][0] - m)
            pn = jnp.exp(scores[gi][1] - m)
            den = den + jnp.sum(p, axis=-1, keepdims=True) + jnp.sum(pn, axis=-1, keepdims=True)
            vt = cache_refs[gi][0, 1, 2 * hp:2 * hp + 2].reshape(128, WINDOWS[gi]).astype(BF16)
            vn = new_refs[gi][0, :, HEAD_LANES + hp * 128:HEAD_LANES + (hp + 1) * 128].astype(BF16)
            o = o + lax.dot_general(p.astype(BF16), vt, _NT, preferred_element_type=F32)
            o = o + jnp.dot(pn.astype(BF16), vn, preferred_element_type=F32)
        o = o / den
        attn_ref[0, :, ch] = jnp.where(hm[0], o[:SAMPLE_T], o[SAMPLE_T:])


def _attn_sample(q, kv_new, caches_t):
    B = q.shape[0]
    specs = []
    for gi, c in enumerate(caches_t):
        assert c.shape[1:] == (2, HEADS, HEAD_DIM, WINDOWS[gi]), "decode caches must hold a full window"
        specs.append(pl.BlockSpec((1,) + c.shape[1:], lambda b: (b, 0, 0, 0, 0)))
    new_spec = pl.BlockSpec((1, SAMPLE_T, KV_WIDTH), lambda b: (b, 0, 0))
    return pl.pallas_call(
        _attn_sample_body,
        grid=(B,),
        in_specs=[pl.BlockSpec((1, SAMPLE_T, Q_WIDTH), lambda b: (b, 0, 0)), new_spec, new_spec, new_spec, *specs],
        out_specs=pl.BlockSpec((1, SAMPLE_T, HEAD_LANES), lambda b: (b, 0, 0)),
        out_shape=jax.ShapeDtypeStruct((B, SAMPLE_T, HEAD_LANES), F32),
        compiler_params=_params(("parallel",)),
        name="attn_sample",
    )(q, *kv_new, *caches_t)


def _merge_body(hs_ref, attn_ref, ga_ref, gb_ref, x_ref, wa_ref, wb_ref, wp_ref, wo_ref, gf_ref, wq_ref, sk_ref,
                x1_ref, xn_ref, st_ref, *, nb, ts):
    rows = nb * ts
    hs = jnp.concatenate([hs_ref[:, k * D_MODEL:(k + 1) * D_MODEL] for k in range(nb)], axis=0).astype(BF16)
    dot = lambda a, w_ref: jnp.dot(a, w_ref[...], preferred_element_type=F32)
    branch_a = dot(hs, wa_ref) * jax.nn.sigmoid(dot(hs, wb_ref))
    branch_b = dot(attn_ref[...].reshape(rows, HEAD_LANES).astype(BF16), wp_ref)
    ga = ga_ref[...].reshape(rows, D_MODEL)
    gb = gb_ref[...].reshape(rows, D_MODEL)
    mix = jax.nn.sigmoid(ga) * branch_a + jax.nn.sigmoid(gb) * branch_b
    x1 = x_ref[...].reshape(rows, D_MODEL) + dot(mix.astype(BF16), wo_ref)
    x1_ref[...] = x1
    ms = jnp.mean(x1 * x1, axis=-1, keepdims=True)
    xn = ((x1 * lax.rsqrt(ms + EPS)) * gf_ref[...]).astype(BF16)
    xn_ref[...] = xn
    qp = dot(xn, wq_ref).astype(BF16)
    for hp in range(2 * PEER_HEADS):
        st_ref[hp] = lax.dot_general(sk_ref[hp], qp[:, hp * PEER_HALF:(hp + 1) * PEER_HALF], _NT,
                                     preferred_element_type=F32)


def _merge(hs_t, attn, ga, gb, x, wa, wb, wp, wo, g_ffn, wq, sk, *, nb, ts):
    B, S, _ = x.shape
    rows = nb * ts
    T = B * S
    nsb = S // ts
    row = lambda w: pl.BlockSpec((nb, ts, w), lambda b, s: (b, s, 0))
    flat = lambda w: pl.BlockSpec((rows, w), lambda b, s: (b * nsb + s, 0))
    return pl.pallas_call(
        functools.partial(_merge_body, nb=nb, ts=ts),
        grid=(B // nb, nsb),
        in_specs=[pl.BlockSpec((ts, nb * D_MODEL), lambda b, s: (s, b)), row(HEAD_LANES), row(D_MODEL),
                  row(D_MODEL), row(D_MODEL),
                  _const_spec((D_MODEL, D_MODEL)), _const_spec((D_MODEL, D_MODEL)),
                  _const_spec((HEAD_LANES, D_MODEL)), _const_spec((D_MODEL, D_MODEL)),
                  _const_spec((1, D_MODEL)), _const_spec((D_MODEL, 2 * PEER_HEADS * PEER_HALF)),
                  _const_spec((2 * PEER_HEADS, N_KEYS, PEER_HALF))],
        out_specs=(flat(D_MODEL), flat(D_MODEL),
                   pl.BlockSpec((2 * PEER_HEADS, N_KEYS, rows), lambda b, s: (0, 0, b * nsb + s))),
        out_shape=(jax.ShapeDtypeStruct((T, D_MODEL), F32), jax.ShapeDtypeStruct((T, D_MODEL), BF16),
                   jax.ShapeDtypeStruct((2 * PEER_HEADS, N_KEYS, T), F32)),
        compiler_params=_params(("parallel", "parallel")),
        name="merge",
    )(hs_t, attn, ga, gb, x, wa, wb, wp, wo, g_ffn.reshape(1, D_MODEL), wq, sk)


ROUTE_LANES = 128
_PAIR_SLABS = ((0, 0, 0, 1, 8), (0, 0, 8, 1, 8), (1, 0, 0, 1, 8), (2, 0, 0, 1, 5), (3, 0, 0, 1, 4),
               (4, 0, 0, 1, 3), (5, 0, 0, 1, 2), (6, 0, 0, 1, 2), (7, 0, 0, 1, 2), (8, 1, 0, 0, 8))


def _top16(s, break_ties):
    idx = lax.broadcasted_iota(jnp.int32, s.shape, 0).astype(F32)
    cur = s
    rank = jnp.full(s.shape, float(PEER_TOPK), F32)
    vals = []
    for r in range(PEER_TOPK):
        m = jnp.max(cur, axis=0, keepdims=True)
        sel = cur == m
        if break_ties:
            sel = idx == jnp.min(jnp.where(sel, idx, float(N_KEYS)), axis=0, keepdims=True)
        rank = jnp.where(sel, float(r), rank)
        cur = jnp.where(sel, NEG_INF, cur)
        vals.append(m)
    n_ranked = jnp.sum(jnp.where(rank < float(PEER_TOPK), 1.0, 0.0), axis=0, keepdims=True)
    return rank, vals, n_ranked


def _route_head(s0, s1, break_ties):
    sub = lax.broadcasted_iota(jnp.int32, (8, ROUTE_LANES), 0)
    rank0, v0, n0 = _top16(s0, break_ties)
    rank1, v1, n1 = _top16(s1, break_ties)
    v0b = jnp.concatenate(v0[8:], axis=0)
    v1a, v1b = jnp.concatenate(v1[:8], axis=0), jnp.concatenate(v1[8:], axis=0)

    cands, cidx = [], []
    for (r0, r0s, r1, r1s, nrow) in _PAIR_SLABS:
        a = (v0b if r0 == 8 else v0[r0])
        b = (v1[0] if r1s == 0 else (v1a if r1 == 0 else v1b))
        c = a + b
        if nrow < 8:
            c = jnp.where(sub < nrow, c, NEG_INF)
        cands.append(c)
        cidx.append(((r0 + r0s * sub) * PEER_TOPK + (r1 + r1s * sub)).astype(F32))

    cur = list(cands)
    picked = [jnp.zeros((8, ROUTE_LANES), F32) for _ in cands]
    big = float(PEER_TOPK * PEER_TOPK)
    for _ in range(PEER_TOPK):
        m = jnp.max(functools.reduce(jnp.maximum, cur), axis=0, keepdims=True)
        sels = [c == m for c in cur]
        if break_ties:
            first = functools.reduce(jnp.minimum, [jnp.where(s, ci, big) for s, ci in zip(sels, cidx)])
            first = jnp.min(first, axis=0, keepdims=True)
            sels = [ci == first for ci in cidx]
        for k, sel in enumerate(sels):
            picked[k] = jnp.where(sel, 1.0, picked[k])
            cur[k] = jnp.where(sel, NEG_INF, cur[k])

    total = functools.reduce(lambda x, y: x + y, picked)
    n_pairs = jnp.sum(total, axis=0, keepdims=True)
    top = v0[0] + v1[0]
    z = functools.reduce(lambda x, y: x + y,
                         [jnp.sum(pk * jnp.exp(c - top), axis=0, keepdims=True) for pk, c in zip(picked, cands)])
    counts = [jnp.sum(picked[0] + picked[1], axis=0, keepdims=True)]
    counts += [jnp.sum(picked[k], axis=0, keepdims=True) for k in range(2, 9)]
    counts += [picked[9][i:i + 1, :] for i in range(8)]
    l0 = jnp.zeros((N_KEYS, ROUTE_LANES), F32)
    for r in range(PEER_TOPK):
        l0 = jnp.where(rank0 == float(r), counts[r], l0)

    k16 = float(PEER_TOPK)
    n_bad = jnp.abs(n0 - k16) + jnp.abs(n1 - k16) + jnp.abs(n_pairs - k16)
    return jnp.exp(s0 - v0[0]) / z, l0, jnp.exp(s1 - v1[0]), rank1, n_bad


ROUTE_HEADS_PER_ITER = 4


def _bf16_pair_words(x):
    w = lax.bitcast_convert_type(x.astype(BF16).astype(F32), jnp.uint32)
    return w | (w >> 16)


def _route_body(st_ref, c0_ref, l0_ref, b1_ref, r1_ref):
    def heads(it, carry):
        def run(break_ties):
            n_bad = None
            for k in range(ROUTE_HEADS_PER_ITER):
                h = it * ROUTE_HEADS_PER_ITER + k
                c0, l0, b1, r1, bad = _route_head(st_ref[2 * h], st_ref[2 * h + 1], break_ties)
                c0_ref[h] = _bf16_pair_words(c0)
                l0_ref[h] = _bf16_pair_words(l0)
                b1_ref[h] = b1.astype(BF16)
                r1_ref[h] = r1.astype(BF16)
                n_bad = bad if n_bad is None else n_bad + bad
            return jnp.max(n_bad)

        n_bad = run(False)

        @pl.when(n_bad > 0.0)
        def _():
            run(True)

        return carry

    lax.fori_loop(0, PEER_HEADS // ROUTE_HEADS_PER_ITER, heads, 0)


def _route(st):
    T = st.shape[2]
    out = lambda dt: jax.ShapeDtypeStruct((PEER_HEADS, N_KEYS, T), dt)
    spec = pl.BlockSpec((PEER_HEADS, N_KEYS, ROUTE_LANES), lambda i: (0, 0, i))
    return pl.pallas_call(
        _route_body,
        grid=(T // ROUTE_LANES,),
        in_specs=[pl.BlockSpec((2 * PEER_HEADS, N_KEYS, ROUTE_LANES), lambda i: (0, 0, i))],
        out_specs=(spec, spec, spec, spec),
        out_shape=(out(jnp.uint32), out(jnp.uint32), out(BF16), out(BF16)),
        compiler_params=_params(("parallel",)),
        name="peer_route",
    )(st)


PEER_EXPERT_BLOCK = 1024
PEER_I_PER_BLOCK = PEER_EXPERT_BLOCK // N_KEYS
GELU_C = math.sqrt(2.0 / math.pi)


def _peer_body(xn_ref, u_ref, vt_ref, c0_ref, l0_ref, b1_ref, r1_ref, x1_ref, gfin_ref, y_ref, acc, pg):
    e = pl.program_id(1)

    @pl.when(e == 0)
    def _():
        acc[...] = jnp.zeros_like(acc)

    tt = xn_ref.shape[0]
    tile = (N_KEYS // BF16_ROWS, BF16_ROWS, tt)
    zero = jnp.zeros(tile, BF16)

    def row16(ref, h, ib):
        return pltpu.bitcast(jnp.broadcast_to(ref[h, ib:ib + 1, :], (8, tt)), BF16)

    def gated(ib, a):
        g = None
        for h in range(PEER_HEADS):
            term = jnp.where(r1_ref[h].reshape(tile) < row16(l0_ref, h, ib)[None], b1_ref[h].reshape(tile), zero)
            term = term * row16(c0_ref, h, ib)[None]
            g = term if g is None else g + term
        x = a.astype(BF16).reshape(tile)
        z = (x * x * (GELU_C * 0.044715) + GELU_C) * x
        hx = 0.5 * x
        return (g * (hx + hx * jnp.tanh(z))).reshape(N_KEYS, tt)

    act = lax.dot_general(u_ref[...], xn_ref[...], _NT, preferred_element_type=F32)
    for ib in range(PEER_I_PER_BLOCK):
        rows = slice(ib * N_KEYS, (ib + 1) * N_KEYS)
        pg[rows, :] = gated(ib, act[rows, :])
    acc[...] += jnp.dot(vt_ref[...], pg[...], preferred_element_type=F32)

    @pl.when(e == pl.num_programs(1) - 1)
    def _():
        x2 = x1_ref[...] + acc[...].T
        ms = jnp.mean(x2 * x2, axis=-1, keepdims=True)
        y_ref[...] = (x2 * lax.rsqrt(ms + EPS)) * gfin_ref[...]


def _peer(xn, u_bf, vt_bf, c0, l0, b1, r1, x1, g_final, *, tt):
    T = xn.shape[0]
    tok = pl.BlockSpec((tt, D_MODEL), lambda t, e: (t, 0))
    rt = pl.BlockSpec((PEER_HEADS, N_KEYS, tt), lambda t, e: (0, 0, t))
    rows = pl.BlockSpec((PEER_HEADS, PEER_I_PER_BLOCK, tt), lambda t, e: (0, e, t))
    return pl.pallas_call(
        _peer_body,
        grid=(T // tt, N_EXPERTS // PEER_EXPERT_BLOCK),
        in_specs=[tok, pl.BlockSpec((PEER_EXPERT_BLOCK, D_MODEL), lambda t, e: (e, 0)),
                  pl.BlockSpec((D_MODEL, PEER_EXPERT_BLOCK), lambda t, e: (0, e)),
                  rows, rows, rt, rt, tok, _const_spec((1, D_MODEL))],
        out_specs=tok,
        out_shape=jax.ShapeDtypeStruct((T, D_MODEL), F32),
        scratch_shapes=[pltpu.VMEM((D_MODEL, tt), F32), pltpu.VMEM((PEER_EXPERT_BLOCK, tt), BF16)],
        compiler_params=_params(("parallel", "arbitrary")),
        name="peer_dense",
    )(xn, u_bf, vt_bf, c0, l0, b1, r1, x1, g_final.reshape(1, D_MODEL))


def _pack_w_in(w_in):
    o_k = D_MODEL + Q_WIDTH
    o_v = o_k + Q_WIDTH
    o_g = o_v + Q_WIDTH
    parts = [w_in[:, :o_k]]
    for gi in range(N_GROUPS):
        parts.append(w_in[:, o_k + gi * HEAD_LANES:o_k + (gi + 1) * HEAD_LANES])
        parts.append(w_in[:, o_v + gi * HEAD_LANES:o_v + (gi + 1) * HEAD_LANES])
    parts.append(w_in[:, o_g:])
    return jnp.concatenate(parts, axis=1).astype(BF16)


def _layer(x, h0_re, h0_im, caches, wts, *, nb, ts, ssm_ts, peer_tt):
    B, S, _ = x.shape
    proj = _in_proj(x, wts["g_mix"], wts["w_in"], nb=nb, ts=ts, dilated=caches is None)
    u_t, kvs, (ga, gb) = proj[0], proj[-5:-2], proj[-2:]
    hs_t, hre, him = _ssm(u_t, h0_re, h0_im, *wts["ssm"], ts=ssm_ts)
    if caches is None:
        outs = [_attn_prompt(proj[1 + gi], proj[1 + N_GROUPS + gi], gi) for gi in range(N_GROUPS)]
        attn = _combine([o for o, _ in outs], [l for _, l in outs], tm=2 * ts)
    else:
        attn = _attn_sample(proj[1], kvs, caches)
    x1, xn, st = _merge(hs_t, attn, ga, gb, x, wts["w_glu_a"], wts["w_glu_b"], wts["w_attn_proj"],
                        wts["w_out"], wts["g_ffn"], wts["w_qp"], wts["sub_keys"], nb=nb, ts=ts)
    c0, l0, b1, r1 = _route(st)
    y = _peer(xn, wts["u_tab"], wts["v_tab_t"], c0, l0, b1, r1, x1, wts["g_final"], tt=peer_tt)
    return y.reshape(B, S, D_MODEL), hre, him, kvs


def kernel(x_prompt, x_sample, state_ssm_re, state_ssm_im, cache_kv_w128, cache_kv_w512, cache_kv_w2048, g_mix, w_in, lam_re, lam_im, log_dt, b_re, b_im, c_re, c_im, d_skip, w_glu_a, w_glu_b, w_attn_proj, w_out, g_ffn, w_qp, sub_keys, u_tab, v_tab, g_final):
    assert w_in.shape[0] == 1, "single-layer model"
    wts = {
        "g_mix": g_mix[0],
        "w_in": _pack_w_in(w_in[0]),
        "ssm": _ssm_weights(lam_re[0], lam_im[0], log_dt[0], b_re[0], b_im[0], c_re[0], c_im[0], d_skip[0]),
        "w_glu_a": w_glu_a[0].astype(BF16),
        "w_glu_b": w_glu_b[0].astype(BF16),
        "w_attn_proj": w_attn_proj[0].astype(BF16),
        "w_out": w_out[0].astype(BF16),
        "g_ffn": g_ffn[0],
        "w_qp": w_qp[0].astype(BF16),
        "sub_keys": sub_keys[0].reshape(2 * PEER_HEADS, N_KEYS, PEER_HALF).astype(BF16),
        "u_tab": u_tab[0].astype(BF16),
        "v_tab_t": v_tab[0].astype(BF16).T,
        "g_final": g_final,
    }
    Bp, Sp, _ = x_prompt.shape
    Bs, Ss, _ = x_sample.shape
    assert Ss == SAMPLE_T
    zeros = jnp.zeros((Bp, SSM_LANES), F32)
    yp, hre_p, him_p, kv_p = _layer(x_prompt, zeros, zeros, None, wts, nb=1, ts=256, ssm_ts=32, peer_tt=1024)
    caches = tuple(c[0].transpose(0, 2, 3, 4, 1) for c in (cache_kv_w128, cache_kv_w512, cache_kv_w2048))
    ys, hre_s, him_s, kv_s = _layer(x_sample, state_ssm_re[0].reshape(Bs, SSM_LANES),
                                    state_ssm_im[0].reshape(Bs, SSM_LANES), caches, wts,
                                    nb=16, ts=SAMPLE_T, ssm_ts=SAMPLE_T, peer_tt=1024)

    state = lambda h: h.reshape(1, -1, SSM_GROUPS, SSM_STATE)
    kv_rows = lambda kv, w: kv[:, kv.shape[1] - min(w, kv.shape[1]):].reshape(1, kv.shape[0], -1, 2, HEADS, HEAD_DIM)
    return (yp, ys, state(hre_p), state(him_p),
            kv_rows(kv_p[0], WINDOWS[0]), kv_rows(kv_p[1], WINDOWS[1]), kv_rows(kv_p[2], WINDOWS[2]),
            state(hre_s), state(him_s),
            kv_rows(kv_s[0], SAMPLE_T), kv_rows(kv_s[1], SAMPLE_T), kv_rows(kv_s[2], SAMPLE_T))
```
